```python
import jax, jax.numpy as jnp
from jax import lax
import numpy as np

D_MODEL = 2048
BATCH = 8
SEQ = 2048
DEPTH = 4

N_META = 16
BLOCK = 128
EPS = 1e-6
MLA_HEADS = 8
Q_LORA = 512
KV_LORA = 256
QK_NOPE = 128
QK_ROPE = 64
V_DIM = 128
ROPE_THETA = 10000.0
SWA_HEADS = 8
SWA_KV_HEADS = 2
SWA_DIM = 128
WINDOW = 128
FOX_HEADS = 16
FOX_DIM = 128
D_FF = 7168
N_EXPERTS = 8
TOP_K = 2
MOE_BLOCK = 256

N_EVEN = (DEPTH + 1) // 2
N_ODD = DEPTH // 2
EVEN_IN = Q_LORA + KV_LORA + QK_ROPE + SWA_HEADS * SWA_DIM + 2 * SWA_KV_HEADS * SWA_DIM
EVEN_OUT = MLA_HEADS * V_DIM + SWA_HEADS * SWA_DIM
ODD_IN = 3 * FOX_HEADS * FOX_DIM + FOX_HEADS
ODD_OUT = FOX_HEADS * FOX_DIM

kernel_name = 'hybrid_mla_swa_fox_moe_trunk'


def rmsnorm(x, g):
    xf = x.astype(jnp.float32)
    y = xf * lax.rsqrt(jnp.mean(xf * xf, axis=-1, keepdims=True) + EPS)
    return (y * g.astype(jnp.float32)).astype(x.dtype)


def rope(x, pos):
    half = x.shape[-1] // 2
    inv = ROPE_THETA ** (-jnp.arange(half, dtype=jnp.float32) / half)
    ang = pos[:, None] * inv[None, :]
    cos = jnp.cos(ang)[None, :, None, :]
    sin = jnp.sin(ang)[None, :, None, :]
    xf = x.astype(jnp.float32)
    x1, x2 = xf[..., :half], xf[..., half:]
    return jnp.concatenate([x1 * cos - x2 * sin, x2 * cos + x1 * sin], axis=-1).astype(x.dtype)


def alibi_slopes(n):
    return jnp.asarray([2.0 ** (-8.0 * (i + 1) / n) for i in range(n)], dtype=jnp.float32)


def blocked_causal_attention(q, k, v, scale, decay_cum=None):
    B, L, H, dk = q.shape
    dv = v.shape[-1]
    n_real = L - N_META
    nb = n_real // BLOCK
    k_pos = jnp.arange(L)
    c_k = None if decay_cum is None else jnp.swapaxes(decay_cum, 1, 2)

    def attend(q_blk, q_pos, c_q):
        s = jnp.einsum('bqhd,bkhd->bhqk', q_blk, k, preferred_element_type=jnp.float32) * scale
        if c_q is not None:
            s = s + (jnp.swapaxes(c_q, 1, 2)[..., None] - c_k[:, :, None, :])
        s = jnp.where(k_pos[None, :] <= q_pos[:, None], s, -jnp.inf)
        p = jax.nn.softmax(s, axis=-1).astype(v.dtype)
        return jnp.einsum('bhqk,bkhd->bqhd', p, v)

    q_pos_m = jnp.arange(N_META)
    qb = jnp.swapaxes(q[:, N_META:].reshape(B, nb, BLOCK, H, dk), 0, 1)
    qp = (N_META + jnp.arange(n_real)).reshape(nb, BLOCK)
    if decay_cum is None:
        out_m = attend(q[:, :N_META], q_pos_m, None)
        out_r = lax.map(lambda a: attend(a[0], a[1], None), (qb, qp))
    else:
        out_m = attend(q[:, :N_META], q_pos_m, decay_cum[:, :N_META])
        cb = jnp.swapaxes(decay_cum[:, N_META:].reshape(B, nb, BLOCK, H), 0, 1)
        out_r = lax.map(lambda a: attend(a[0], a[1], a[2]), (qb, qp, cb))
    out_r = jnp.swapaxes(out_r, 0, 1).reshape(B, n_real, H, dv)
    return jnp.concatenate([out_m, out_r], axis=1)


def sliding_window_sink_attention(q, k, v, sinks, slopes):
    B, L, Hq, d = q.shape
    Hkv = k.shape[2]
    G = Hq // Hkv
    n_real = L - N_META
    nb = n_real // BLOCK
    scale = d ** -0.5
    sink = sinks.astype(jnp.float32).reshape(Hkv, G)
    slope = slopes.reshape(Hkv, G)

    def attend(q_b, q_pos, k_b, v_b, k_pos, k_valid):
        s = jnp.einsum('bnqhgd,bnkhd->bnhgqk', q_b, k_b, preferred_element_type=jnp.float32) * scale
        dist = q_pos[:, :, None] - k_pos[:, None, :]
        allowed = k_valid[:, None, :] & (dist >= 0) & ((dist < WINDOW) | (k_pos[:, None, :] < N_META))
        s = s - slope[None, None, :, :, None, None] * jnp.abs(dist).astype(jnp.float32)[None, :, None, None]
        s = jnp.where(allowed[None, :, None, None], s, -jnp.inf)
        sink_col = jnp.broadcast_to(sink[None, None, :, :, None, None], s.shape[:-1] + (1,))
        p = jax.nn.softmax(jnp.concatenate([s, sink_col], axis=-1), axis=-1)[..., :-1]
        return jnp.einsum('bnhgqk,bnkhd->bnqhgd', p.astype(v_b.dtype), v_b)

    meta_pos = jnp.arange(N_META)
    out_m = attend(q[:, :N_META].reshape(B, 1, N_META, Hkv, G, d), meta_pos[None],
                   k[:, None, :N_META], v[:, None, :N_META], meta_pos[None],
                   jnp.ones((1, N_META), dtype=bool))
    q_r = q[:, N_META:].reshape(B, nb, BLOCK, Hkv, G, d)

    def band(t):
        t_r = t[:, N_META:].reshape(B, nb, BLOCK, Hkv, d)
        t_prev = jnp.concatenate([jnp.zeros_like(t_r[:, :1]), t_r[:, :-1]], axis=1)
        t_meta = jnp.broadcast_to(t[:, None, :N_META], (B, nb, N_META, Hkv, d))
        return jnp.concatenate([t_meta, t_prev, t_r], axis=2)

    blk_start = N_META + jnp.arange(nb) * BLOCK
    ar = jnp.arange(BLOCK)
    q_pos_r = blk_start[:, None] + ar[None, :]
    k_pos_r = jnp.concatenate([jnp.broadcast_to(meta_pos[None], (nb, N_META)),
                               blk_start[:, None] - BLOCK + ar[None, :],
                               q_pos_r], axis=1)
    k_valid_r = jnp.concatenate([jnp.ones((nb, N_META), dtype=bool),
                                 jnp.broadcast_to((jnp.arange(nb) > 0)[:, None], (nb, BLOCK)),
                                 jnp.ones((nb, BLOCK), dtype=bool)], axis=1)
    out_r = attend(q_r, q_pos_r, band(k), band(v), k_pos_r, k_valid_r)
    return jnp.concatenate([out_m.reshape(B, N_META, Hq, d), out_r.reshape(B, n_real, Hq, d)], axis=1)


def even_mixer(h, pos, slopes, w_in, q_norm, kv_norm, w_uq, w_ukv, sinks, w_o):
    B, L, _ = h.shape
    proj = h @ w_in
    cuts = [Q_LORA, Q_LORA + KV_LORA, Q_LORA + KV_LORA + QK_ROPE,
            Q_LORA + KV_LORA + QK_ROPE + SWA_HEADS * SWA_DIM,
            Q_LORA + KV_LORA + QK_ROPE + (SWA_HEADS + SWA_KV_HEADS) * SWA_DIM]
    c_q, c_kv, k_r, q_s, k_s, v_s = jnp.split(proj, cuts, axis=-1)
    qa = (rmsnorm(c_q, q_norm) @ w_uq).reshape(B, L, MLA_HEADS, QK_NOPE + QK_ROPE)
    qa = jnp.concatenate([qa[..., :QK_NOPE], rope(qa[..., QK_NOPE:], pos)], axis=-1)
    kv = (rmsnorm(c_kv, kv_norm) @ w_ukv).reshape(B, L, MLA_HEADS, QK_NOPE + V_DIM)
    k_rope = rope(k_r[:, :, None, :], pos)
    ka = jnp.concatenate([kv[..., :QK_NOPE],
                          jnp.broadcast_to(k_rope, (B, L, MLA_HEADS, QK_ROPE))], axis=-1)
    o_a = blocked_causal_attention(qa, ka, kv[..., QK_NOPE:], (QK_NOPE + QK_ROPE) ** -0.5)
    o_b = sliding_window_sink_attention(q_s.reshape(B, L, SWA_HEADS, SWA_DIM),
                                        k_s.reshape(B, L, SWA_KV_HEADS, SWA_DIM),
                                        v_s.reshape(B, L, SWA_KV_HEADS, SWA_DIM), sinks, slopes)
    o = jnp.concatenate([o_a.reshape(B, L, -1), o_b.reshape(B, L, -1)], axis=-1)
    return o @ w_o


def fox_mixer(h, w_in, b_f, w_o):
    B, L, _ = h.shape
    proj = h @ w_in
    hd = FOX_HEADS * FOX_DIM
    q = proj[..., :hd].reshape(B, L, FOX_HEADS, FOX_DIM)
    k = proj[..., hd:2 * hd].reshape(B, L, FOX_HEADS, FOX_DIM)
    v = proj[..., 2 * hd:3 * hd].reshape(B, L, FOX_HEADS, FOX_DIM)
    log_f = jax.nn.log_sigmoid(proj[..., 3 * hd:].astype(jnp.float32) + b_f.astype(jnp.float32))
    c = lax.cumsum(log_f, axis=1)
    o = blocked_causal_attention(q, k, v, FOX_DIM ** -0.5, c)
    return o.reshape(B, L, -1) @ w_o


def swiglu(h, w_gate_up, w_down):
    gu = h @ w_gate_up
    return (jax.nn.silu(gu[..., :D_FF]) * gu[..., D_FF:]) @ w_down


def moe_swiglu(h, router, w_gate, w_up, w_down):
    B, L, D = h.shape
    xt = h.reshape(-1, D)
    N = xt.shape[0]
    logits = (xt @ router).astype(jnp.float32)
    top_val, top_idx = lax.top_k(logits, TOP_K)
    gates = jax.nn.softmax(top_val, axis=-1)
    A = N * TOP_K
    flat_e = top_idx.reshape(-1)
    flat_tok = jnp.arange(A) // TOP_K
    order = jnp.argsort(flat_e)
    se, st, sg = flat_e[order], flat_tok[order], gates.reshape(-1)[order]
    counts = jnp.bincount(flat_e, length=N_EXPERTS)
    padded = (counts + MOE_BLOCK - 1) // MOE_BLOCK * MOE_BLOCK
    start_sorted = jnp.cumsum(counts) - counts
    end_pad = jnp.cumsum(padded)
    start_pad = end_pad - padded
    dest = start_pad[se] + jnp.arange(A) - start_sorted[se]
    n_blocks = -(-A // MOE_BLOCK) + N_EXPERTS
    P = n_blocks * MOE_BLOCK
    row_tok = jnp.full((P,), N, dtype=jnp.int32).at[dest].set(st.astype(jnp.int32))
    row_gate = jnp.zeros((P,), jnp.float32).at[dest].set(sg)
    blk_start = jnp.arange(n_blocks) * MOE_BLOCK
    blk_e = jnp.minimum(jnp.sum(blk_start[:, None] >= end_pad[None, :], axis=1), N_EXPERTS - 1)
    x_pad = jnp.concatenate([xt, jnp.zeros((1, D), xt.dtype)], axis=0)

    def expert_block(args):
        tok, e = args
        xb = x_pad[tok]
        return (jax.nn.silu(xb @ w_gate[e]) * (xb @ w_up[e])) @ w_down[e]

    y = lax.map(expert_block, (row_tok.reshape(n_blocks, MOE_BLOCK), blk_e))
    y = y.reshape(P, D) * row_gate[:, None].astype(y.dtype)
    out = jnp.zeros((N + 1, D), y.dtype).at[row_tok].add(y)[:N]
    return out.reshape(B, L, D)


def setup_inputs(seed: int = 0) -> dict:
    key = jax.random.key(seed)
    ks = iter(jax.random.split(key, 32))
    f32 = jnp.float32

    def nrm(shape, scale):
        return jax.random.normal(next(ks), shape, f32) * scale

    def gain(shape):
        return 1.0 + 0.02 * jax.random.normal(next(ks), shape, f32)

    D = D_MODEL
    return {
        'x': nrm((BATCH, SEQ, D), 1.0),
        'meta': nrm((N_META, D), 1.0),
        'ev_norm1': gain((N_EVEN, D)),
        'ev_w_in': nrm((N_EVEN, D, EVEN_IN), D ** -0.5),
        'ev_q_norm': gain((N_EVEN, Q_LORA)),
        'ev_kv_norm': gain((N_EVEN, KV_LORA)),
        'ev_w_uq': nrm((N_EVEN, Q_LORA, MLA_HEADS * (QK_NOPE + QK_ROPE)), Q_LORA ** -0.5),
        'ev_w_ukv': nrm((N_EVEN, KV_LORA, MLA_HEADS * (QK_NOPE + V_DIM)), KV_LORA ** -0.5),
        'ev_sinks': nrm((N_EVEN, SWA_HEADS), 0.5),
        'ev_w_o': nrm((N_EVEN, EVEN_OUT, D), EVEN_OUT ** -0.5),
        'ev_norm2': gain((N_EVEN, D)),
        'ev_w_gate_up': nrm((N_EVEN, D, 2 * D_FF), D ** -0.5),
        'ev_w_down': nrm((N_EVEN, D_FF, D), D_FF ** -0.5),
        'od_norm1': gain((N_ODD, D)),
        'od_w_in': nrm((N_ODD, D, ODD_IN), D ** -0.5),
        'od_b_f': 3.0 + nrm((N_ODD, FOX_HEADS), 0.5),
        'od_w_o': nrm((N_ODD, ODD_OUT, D), ODD_OUT ** -0.5),
        'od_norm2': gain((N_ODD, D)),
        'od_router': nrm((N_ODD, D, N_EXPERTS), D ** -0.5),
        'od_w_gate': nrm((N_ODD, N_EXPERTS, D, D_FF), D ** -0.5),
        'od_w_up': nrm((N_ODD, N_EXPERTS, D, D_FF), D ** -0.5),
        'od_w_down': nrm((N_ODD, N_EXPERTS, D_FF, D), D_FF ** -0.5),
        'final_norm': gain((D,)),
    }


def reference(x, meta, ev_norm1, ev_w_in, ev_q_norm, ev_kv_norm, ev_w_uq, ev_w_ukv, ev_sinks,
              ev_w_o, ev_norm2, ev_w_gate_up, ev_w_down, od_norm1, od_w_in, od_b_f, od_w_o,
              od_norm2, od_router, od_w_gate, od_w_up, od_w_down, final_norm):
    B = x.shape[0]
    h = jnp.concatenate([jnp.broadcast_to(meta[None].astype(x.dtype), (B, N_META, D_MODEL)), x], axis=1)
    L = h.shape[1]
    pos = jnp.arange(L, dtype=jnp.float32)
    slopes = alibi_slopes(SWA_HEADS)
    for layer in range(DEPTH):
        i = layer // 2
        if layer % 2 == 0:
            h = h + even_mixer(rmsnorm(h, ev_norm1[i]), pos, slopes, ev_w_in[i], ev_q_norm[i],
                               ev_kv_norm[i], ev_w_uq[i], ev_w_ukv[i], ev_sinks[i], ev_w_o[i])
            h = h + swiglu(rmsnorm(h, ev_norm2[i]), ev_w_gate_up[i], ev_w_down[i])
        else:
            h = h + fox_mixer(rmsnorm(h, od_norm1[i]), od_w_in[i], od_b_f[i], od_w_o[i])
            h = h + moe_swiglu(rmsnorm(h, od_norm2[i]), od_router[i], od_w_gate[i],
                               od_w_up[i], od_w_down[i])
    return rmsnorm(h[:, N_META:], final_norm)
```

```python
import functools

import jax
import jax.numpy as jnp
from jax import lax
from jax.experimental import pallas as pl
from jax.experimental.pallas import tpu as pltpu

F32 = jnp.float32
BF16 = jnp.bfloat16

N_META = 16
BLOCK = 128
EPS = 1e-6
MLA_HEADS = 8
Q_LORA = 512
KV_LORA = 256
QK_NOPE = 128
QK_ROPE = 64
V_DIM = 128
ROPE_THETA = 10000.0
SWA_HEADS = 8
SWA_KV_HEADS = 2
SWA_DIM = 128
SWA_GROUP = SWA_HEADS // SWA_KV_HEADS
FOX_HEADS = 16
FOX_DIM = 128
TOP_K = 2
MOE_ROWS = 512
LANES = 128
NEG = -1e30
V7X_VMEM_BYTES = 64 * 1024 * 1024
VMEM_LIMIT = V7X_VMEM_BYTES - 8 * 1024 * 1024


def _params(*sem):
    return pltpu.CompilerParams(dimension_semantics=sem, vmem_limit_bytes=VMEM_LIMIT)


def _pick(n, candidates):
    for c in candidates:
        if n % c == 0:
            return c
    return n


def _row_tile(n):
    return _pick(n, (688, 512, 384, 256, 128, 64, 32, 16))


def _norm_rows(x, g):
    ms = jnp.mean(x * x, axis=-1, keepdims=True)
    return (x * lax.rsqrt(ms + EPS)) * g


def _norm_proj_kernel(x_ref, g_ref, w_ref, *rest, has_aux):
    if has_aux:
        waux_ref, o_ref, oaux_ref, xn_ref = rest
    else:
        o_ref, xn_ref = rest

    @pl.when(pl.program_id(1) == 0)
    def _():
        xn = _norm_rows(x_ref[...].astype(F32), g_ref[...]).astype(BF16)
        xn_ref[...] = xn
        if has_aux:
            oaux_ref[...] = jnp.dot(xn, waux_ref[...], preferred_element_type=F32)

    o_ref[...] = jnp.dot(xn_ref[...], w_ref[...], preferred_element_type=F32).astype(o_ref.dtype)


def norm_proj(x, xcol, k, gain, w, w_aux=None, out_dtype=BF16, name="norm_proj"):
    n = x.shape[0]
    nout = w.shape[1]
    tm = _row_tile(n)
    tn = _pick(nout, (1024, 512, 256, 128))
    has_aux = w_aux is not None
    in_specs = [
        pl.BlockSpec((tm, k), lambda i, j: (i, xcol)),
        pl.BlockSpec((1, k), lambda i, j: (0, 0)),
        pl.BlockSpec((k, tn), lambda i, j: (0, j)),
    ]
    args = [x, gain.reshape(1, k).astype(F32), w]
    out_shape = [jax.ShapeDtypeStruct((n, nout), out_dtype)]
    out_specs = [pl.BlockSpec((tm, tn), lambda i, j: (i, j))]
    if has_aux:
        na = w_aux.shape[1]
        in_specs.append(pl.BlockSpec((k, na), lambda i, j: (0, 0)))
        args.append(w_aux)
        out_shape.append(jax.ShapeDtypeStruct((n, na), F32))
        out_specs.append(pl.BlockSpec((tm, na), lambda i, j: (i, 0)))
    res = pl.pallas_call(
        functools.partial(_norm_proj_kernel, has_aux=has_aux),
        grid=(n // tm, nout // tn),
        in_specs=in_specs,
        out_specs=out_specs,
        out_shape=out_shape,
        scratch_shapes=[pltpu.VMEM((tm, k), BF16)],
        compiler_params=_params("parallel", "arbitrary"),
        name=name,
    )(*args)
    return res if has_aux else res[0]


def _out_proj_kernel(*refs, n_lhs):
    a_refs = refs[:n_lhs]
    w_refs = refs[n_lhs:2 * n_lhs]
    res_ref, o_ref = refs[2 * n_lhs:]
    acc = res_ref[...]
    for a_ref, w_ref in zip(a_refs, w_refs):
        acc = acc + jnp.dot(a_ref[...], w_ref[...], preferred_element_type=F32)
    o_ref[...] = acc


def out_proj(lhs, ws, res, name="out_proj"):
    n, d = res.shape
    tm = _row_tile(n)
    tn = _pick(d, (1024, 512, 256, 128))
    in_specs = []
    for a in lhs:
        in_specs.append(pl.BlockSpec((tm, a.shape[1]), lambda i, j: (i, 0)))
    for w in ws:
        in_specs.append(pl.BlockSpec((w.shape[0], tn), lambda i, j: (0, j)))
    in_specs.append(pl.BlockSpec((tm, tn), lambda i, j: (i, j)))
    return pl.pallas_call(
        functools.partial(_out_proj_kernel, n_lhs=len(lhs)),
        grid=(n // tm, d // tn),
        in_specs=in_specs,
        out_specs=pl.BlockSpec((tm, tn), lambda i, j: (i, j)),
        out_shape=jax.ShapeDtypeStruct((n, d), F32),
        compiler_params=_params("parallel", "parallel"),
        name=name,
    )(*lhs, *ws, res)


def _norm_router_kernel(x_ref, g_ref, wr_ref, xn_ref, logit_ref):
    xn = _norm_rows(x_ref[...], g_ref[...])
    xn_ref[...] = xn.astype(BF16)
    logit_ref[...] = jnp.dot(xn, wr_ref[...], preferred_element_type=F32,
                             precision=lax.Precision.HIGHEST)


def norm_router(h, gain, router_padded):
    n, d = h.shape
    tm = _row_tile(n)
    ne = router_padded.shape[1]
    return pl.pallas_call(
        _norm_router_kernel,
        grid=(n // tm,),
        in_specs=[pl.BlockSpec((tm, d), lambda i: (i, 0)),
                  pl.BlockSpec((1, d), lambda i: (0, 0)),
                  pl.BlockSpec((d, ne), lambda i: (0, 0))],
        out_specs=[pl.BlockSpec((tm, d), lambda i: (i, 0)),
                   pl.BlockSpec((tm, ne), lambda i: (i, 0))],
        out_shape=[jax.ShapeDtypeStruct((n, d), BF16), jax.ShapeDtypeStruct((n, ne), F32)],
        compiler_params=_params("parallel"),
        name="norm_router",
    )(h, gain.reshape(1, d).astype(F32), router_padded)


def _final_norm_kernel(x_ref, g_ref, o_ref):
    o_ref[...] = _norm_rows(x_ref[...], g_ref[...])


def final_rmsnorm(h, gain):
    n, d = h.shape
    tm = _row_tile(n)
    return pl.pallas_call(
        _final_norm_kernel,
        grid=(n // tm,),
        in_specs=[pl.BlockSpec((tm, d), lambda i: (i, 0)), pl.BlockSpec((1, d), lambda i: (0, 0))],
        out_specs=pl.BlockSpec((tm, d), lambda i: (i, 0)),
        out_shape=jax.ShapeDtypeStruct((n, d), F32),
        compiler_params=_params("parallel"),
        name="final_norm",
    )(h, gain.reshape(1, d).astype(F32))


def _swiglu_part(xn, wg, wu, wd):
    g = jnp.dot(xn, wg, preferred_element_type=F32)
    u = jnp.dot(xn, wu, preferred_element_type=F32)
    a = (g * jax.nn.sigmoid(g) * u).astype(BF16)
    return jnp.dot(a, wd, preferred_element_type=F32)


def _dense_ffn_kernel(x_ref, g_ref, wg_ref, wu_ref, wd_ref, o_ref, xn_ref):
    j = pl.program_id(1)

    @pl.when(j == 0)
    def _():
        xn_ref[...] = _norm_rows(x_ref[...], g_ref[...]).astype(BF16)

    part = _swiglu_part(xn_ref[...], wg_ref[...], wu_ref[...], wd_ref[...])

    @pl.when(j == 0)
    def _():
        o_ref[...] = x_ref[...] + part

    @pl.when(j > 0)
    def _():
        o_ref[...] += part


def dense_ffn(h, gain, w_gate_up, w_down):
    n, d = h.shape
    dff = w_down.shape[0]
    tm = _row_tile(n)
    tf = _pick(dff, (512, 256, 128))
    nj = dff // tf
    return pl.pallas_call(
        _dense_ffn_kernel,
        grid=(n // tm, nj),
        in_specs=[pl.BlockSpec((tm, d), lambda i, j: (i, 0)),
                  pl.BlockSpec((1, d), lambda i, j: (0, 0)),
                  pl.BlockSpec((d, tf), lambda i, j: (0, j)),
                  pl.BlockSpec((d, tf), lambda i, j: (0, nj + j)),
                  pl.BlockSpec((tf, d), lambda i, j: (j, 0))],
        out_specs=pl.BlockSpec((tm, d), lambda i, j: (i, 0)),
        out_shape=jax.ShapeDtypeStruct((n, d), F32),
        scratch_shapes=[pltpu.VMEM((tm, d), BF16)],
        compiler_params=_params("parallel", "arbitrary"),
        name="dense_ffn",
    )(h, gain.reshape(1, d).astype(F32), w_gate_up, w_gate_up, w_down)


def _moe_ffn_kernel(be_ref, nused_ref, x_ref, wg_ref, wu_ref, wd_ref, o_ref):
    b = pl.program_id(0)
    j = pl.program_id(1)
    used = b < nused_ref[0]

    @pl.when(used)
    def _():
        part = _swiglu_part(x_ref[...], wg_ref[0], wu_ref[0], wd_ref[0])

        @pl.when(j == 0)
        def _():
            o_ref[...] = part

        @pl.when(j > 0)
        def _():
            o_ref[...] += part

    @pl.when(jnp.logical_and(jnp.logical_not(used), j == 0))
    def _():
        o_ref[...] = jnp.zeros_like(o_ref)


def moe_ffn(x_sorted, blk_e, n_used, w_gate, w_up, w_down, tm):
    p, d = x_sorted.shape
    dff = w_down.shape[1]
    tf = _pick(dff, (512, 256, 128))
    nj = dff // tf
    nb = p // tm

    def jj(b, j, nused):
        return jnp.where(b < nused[0], j, nj - 1)

    grid_spec = pltpu.PrefetchScalarGridSpec(
        num_scalar_prefetch=2,
        grid=(nb, nj),
        in_specs=[pl.BlockSpec((tm, d), lambda b, j, be, nu: (b, 0)),
                  pl.BlockSpec((1, d, tf), lambda b, j, be, nu: (be[b], 0, jj(b, j, nu))),
                  pl.BlockSpec((1, d, tf), lambda b, j, be, nu: (be[b], 0, jj(b, j, nu))),
                  pl.BlockSpec((1, tf, d), lambda b, j, be, nu: (be[b], jj(b, j, nu), 0))],
        out_specs=pl.BlockSpec((tm, d), lambda b, j, be, nu: (b, 0)),
    )
    return pl.pallas_call(
        _moe_ffn_kernel,
        grid_spec=grid_spec,
        out_shape=jax.ShapeDtypeStruct((p, d), F32),
        compiler_params=_params("parallel", "arbitrary"),
        name="moe_ffn",
    )(blk_e, n_used, x_sorted, w_gate, w_up, w_down)


def _qk(q, k):
    return lax.dot_general(q, k, (((1,), (1,)), ((), ())), preferred_element_type=F32)


def _flash_update(s, v, m, l, acc):
    m_new = jnp.maximum(m, jnp.max(s, axis=-1, keepdims=True))
    alpha = jnp.exp(m - m_new)
    p = jnp.exp(s - m_new)
    l = alpha * l + jnp.sum(p, axis=-1, keepdims=True)
    acc = alpha * acc + jnp.dot(p.astype(BF16), v, preferred_element_type=F32)
    return m_new, l, acc


def _iota2(shape):
    return (lax.broadcasted_iota(jnp.int32, shape, 0), lax.broadcasted_iota(jnp.int32, shape, 1))


def _causal_sweep(q_ref, k_ref, v_ref, o_ref, col_bias, nb):
    r16, c16 = _iota2((N_META, BLOCK))
    r, c = _iota2((BLOCK, BLOCK))
    k0 = k_ref[0:BLOCK, :]
    v0 = v_ref[0:BLOCK, :]

    s = _qk(q_ref[0:N_META, :], k0) + col_bias(-1)
    s = jnp.where(c16 <= r16, s, NEG)
    m = jnp.max(s, axis=-1, keepdims=True)
    p = jnp.exp(s - m)
    o = jnp.dot(p.astype(BF16), v0, preferred_element_type=F32) / jnp.sum(p, axis=-1, keepdims=True)
    o_ref[0:N_META, :] = o.astype(o_ref.dtype)

    def qblock(i, carry):
        qs = pl.multiple_of(N_META + i * BLOCK, N_META)
        q = q_ref[pl.ds(qs, BLOCK), :]
        s = jnp.where(c < N_META, _qk(q, k0) + col_bias(-1), NEG)
        m = jnp.max(s, axis=-1, keepdims=True)
        p = jnp.exp(s - m)
        l = jnp.sum(p, axis=-1, keepdims=True)
        acc = jnp.dot(p.astype(BF16), v0, preferred_element_type=F32)

        def chunk(t, mla):
            ks = pl.multiple_of(N_META + t * BLOCK, N_META)
            s = _qk(q, k_ref[pl.ds(ks, BLOCK), :]) + col_bias(t)
            return _flash_update(s, v_ref[pl.ds(ks, BLOCK), :], *mla)

        m, l, acc = lax.fori_loop(0, i, chunk, (m, l, acc))
        s = _qk(q, k_ref[pl.ds(qs, BLOCK), :]) + col_bias(i)
        s = jnp.where(c <= r, s, NEG)
        m, l, acc = _flash_update(s, v_ref[pl.ds(qs, BLOCK), :], m, l, acc)
        o_ref[pl.ds(qs, BLOCK), :] = (acc / l).astype(o_ref.dtype)
        return carry

    lax.fori_loop(0, nb, qblock, 0)


def _fox_kernel(q_ref, k_ref, v_ref, cm_ref, cr_ref, o_ref, *, nb):
    def col_bias(t):
        if isinstance(t, int):
            return -cm_ref[...]
        return -cr_ref[pl.ds(t, 1), :]

    _causal_sweep(q_ref, k_ref, v_ref, o_ref, col_bias, nb)


def fox_attention(proj, c_meta, c_real, b, l):
    nb = (l - N_META) // BLOCK
    h = FOX_HEADS
    p3 = proj.reshape(b, l, 3 * h * FOX_DIM)
    blk = lambda off: pl.BlockSpec((None, l, FOX_DIM), lambda bi, hi: (bi, 0, off + hi))
    out = pl.pallas_call(
        functools.partial(_fox_kernel, nb=nb),
        grid=(b, h),
        in_specs=[blk(0), blk(h), blk(2 * h),
                  pl.BlockSpec((None, None, 1, BLOCK), lambda bi, hi: (bi, hi, 0, 0)),
                  pl.BlockSpec((None, None, nb, BLOCK), lambda bi, hi: (bi, hi, 0, 0))],
        out_specs=pl.BlockSpec((None, l, FOX_DIM), lambda bi, hi: (bi, 0, hi)),
        out_shape=jax.ShapeDtypeStruct((b, l, h * FOX_DIM), BF16),
        compiler_params=_params("parallel", "parallel"),
        name="fox_attention",
    )(p3, p3, p3, c_meta, c_real)
    return out.reshape(b * l, h * FOX_DIM)


def _mla_kernel(q_ref, kv_ref, kr_ref, cs_ref, o_ref, qx_ref, kx_ref, *, nb):
    cs = cs_ref[...]
    lane = lax.broadcasted_iota(jnp.int32, cs.shape, 1)
    eq = q_ref[:, QK_NOPE:].astype(F32) * cs
    qx_ref[:, :QK_NOPE] = q_ref[:, :QK_NOPE]
    qx_ref[:, QK_NOPE:] = (eq + pltpu.roll(eq, QK_ROPE, axis=1)).astype(BF16)
    ek = kr_ref[...].astype(F32) * cs
    rk = jnp.where(lane < QK_ROPE, ek + pltpu.roll(ek, QK_ROPE, axis=1), 0.0)
    kx_ref[:, :QK_NOPE] = kv_ref[:, :QK_NOPE]
    kx_ref[:, QK_NOPE:] = rk.astype(BF16)
    v_ref = kv_ref.at[:, QK_NOPE:]
    _causal_sweep(qx_ref, kx_ref, v_ref, o_ref, lambda t: 0.0, nb)


def mla_attention(qa, kv, proj, kr_col, cs, b, l):
    nb = (l - N_META) // BLOCK
    h = MLA_HEADS
    w = QK_NOPE + 2 * QK_ROPE
    qa3 = qa.reshape(b, l, h * w)
    kv3 = kv.reshape(b, l, h * (QK_NOPE + V_DIM))
    p3 = proj.reshape(b, l, proj.shape[1])
    out = pl.pallas_call(
        functools.partial(_mla_kernel, nb=nb),
        grid=(b, h),
        in_specs=[pl.BlockSpec((None, l, w), lambda bi, hi: (bi, 0, hi)),
                  pl.BlockSpec((None, l, QK_NOPE + V_DIM), lambda bi, hi: (bi, 0, hi)),
                  pl.BlockSpec((None, l, 2 * QK_ROPE), lambda bi, hi: (bi, 0, kr_col)),
                  pl.BlockSpec((l, 2 * QK_ROPE), lambda bi, hi: (0, 0))],
        out_specs=pl.BlockSpec((None, l, V_DIM), lambda bi, hi: (bi, 0, hi)),
        out_shape=jax.ShapeDtypeStruct((b, l, h * V_DIM), BF16),
        scratch_shapes=[pltpu.VMEM((l, w), BF16), pltpu.VMEM((l, w), BF16)],
        compiler_params=_params("parallel", "parallel"),
        name="mla_attention",
    )(qa3, kv3, p3, cs)
    return out.reshape(b * l, h * V_DIM)


def _swa_kernel(sink_ref, slope_ref, q_ref, k_ref, v_ref, o_ref, *, nb):
    kvh = pl.program_id(1)
    r16, c16 = _iota2((N_META, BLOCK))
    r, c = _iota2((BLOCK, BLOCK))
    rel = (r - c).astype(F32)
    rel16 = (r16 - c16).astype(F32)
    k0 = k_ref[0:BLOCK, :]
    v0 = v_ref[0:BLOCK, :]

    for g in range(SWA_GROUP):
        head = kvh * SWA_GROUP + g
        sink = sink_ref[head]
        slope = slope_ref[head]
        cols = slice(g * SWA_DIM, (g + 1) * SWA_DIM)

        def with_sink(s, v):
            m0 = jnp.full((s.shape[0], 1), sink, F32)
            l0 = jnp.ones((s.shape[0], 1), F32)
            acc0 = jnp.zeros((s.shape[0], SWA_DIM), F32)
            return _flash_update(s, v, m0, l0, acc0)

        s = _qk(q_ref[0:N_META, cols], k0) - slope * rel16
        s = jnp.where(c16 <= r16, s, NEG)
        m, l, acc = with_sink(s, v0)
        o_ref[0:N_META, cols] = (acc / l).astype(o_ref.dtype)

        def qblock(i, carry):
            qs = pl.multiple_of(N_META + i * BLOCK, N_META)
            ps = pl.multiple_of(jnp.where(i > 0, qs - BLOCK, qs), N_META)
            q = q_ref[pl.ds(qs, BLOCK), cols]
            base = (N_META + i * BLOCK).astype(F32)
            s = _qk(q, k0) - slope * (rel + base)
            s = jnp.where(c < N_META, s, NEG)
            m, l, acc = with_sink(s, v0)
            s = _qk(q, k_ref[pl.ds(ps, BLOCK), :]) - slope * (rel + float(BLOCK))
            s = jnp.where(jnp.logical_and(c > r, i > 0), s, NEG)
            m, l, acc = _flash_update(s, v_ref[pl.ds(ps, BLOCK), :], m, l, acc)
            s = _qk(q, k_ref[pl.ds(qs, BLOCK), :]) - slope * rel
            s = jnp.where(c <= r, s, NEG)
            m, l, acc = _flash_update(s, v_ref[pl.ds(qs, BLOCK), :], m, l, acc)
            o_ref[pl.ds(qs, BLOCK), cols] = (acc / l).astype(o_ref.dtype)
            return carry

        lax.fori_loop(0, nb, qblock, 0)


def swa_attention(proj, q_col, k_col, v_col, sinks, slopes, b, l):
    nb = (l - N_META) // BLOCK
    gw = SWA_GROUP * SWA_DIM
    p3 = proj.reshape(b, l, proj.shape[1])
    grid_spec = pltpu.PrefetchScalarGridSpec(
        num_scalar_prefetch=2,
        grid=(b, SWA_KV_HEADS),
        in_specs=[pl.BlockSpec((None, l, gw), lambda bi, hi, s0, s1: (bi, 0, q_col + hi)),
                  pl.BlockSpec((None, l, SWA_DIM), lambda bi, hi, s0, s1: (bi, 0, k_col + hi)),
                  pl.BlockSpec((None, l, SWA_DIM), lambda bi, hi, s0, s1: (bi, 0, v_col + hi))],
        out_specs=pl.BlockSpec((None, l, gw), lambda bi, hi, s0, s1: (bi, 0, hi)),
    )
    out = pl.pallas_call(
        functools.partial(_swa_kernel, nb=nb),
        grid_spec=grid_spec,
        out_shape=jax.ShapeDtypeStruct((b, l, SWA_HEADS * SWA_DIM), BF16),
        compiler_params=_params("parallel", "parallel"),
        name="swa_attention",
    )(sinks.astype(F32), slopes, p3, p3, p3)
    return out.reshape(b * l, SWA_HEADS * SWA_DIM)


def _rope_swap_cols(w):
    half = QK_ROPE // 2
    return jnp.concatenate([-w[..., half:], w[..., :half]], axis=-1)


def _rope_table(l):
    half = QK_ROPE // 2
    inv = ROPE_THETA ** (-jnp.arange(half, dtype=F32) / half)
    ang = jnp.arange(l, dtype=F32)[:, None] * inv[None, :]
    cos, sin = jnp.cos(ang), jnp.sin(ang)
    return jnp.concatenate([cos, cos, sin, sin], axis=-1)


def even_layer(h, b, l, norm1, w_in, q_norm, kv_norm, w_uq, w_ukv, sinks, w_o, norm2,
               w_gate_up, w_down, cs, slopes):
    d = h.shape[1]
    o1, o2, o3 = Q_LORA, Q_LORA + KV_LORA, Q_LORA + KV_LORA + QK_ROPE
    o4 = o3 + SWA_HEADS * SWA_DIM
    o5 = o4 + SWA_KV_HEADS * SWA_DIM
    w_kr = w_in[:, o2:o3]
    w_in2 = jnp.concatenate([w_in[:, o3:o4] * (SWA_DIM ** -0.5), w_in[:, :o1], w_in[:, o1:o2],
                             w_in[:, o4:o5], w_in[:, o5:], w_kr, _rope_swap_cols(w_kr)],
                            axis=1).astype(BF16)
    w_in2 = jnp.pad(w_in2, ((0, 0), (0, -w_in2.shape[1] % 512)))
    proj = norm_proj(h, 0, d, norm1, w_in2, name="even_in_proj")
    cq_col = (SWA_HEADS * SWA_DIM) // Q_LORA
    ckv_col = (SWA_HEADS * SWA_DIM + Q_LORA) // KV_LORA
    ks_col = (SWA_HEADS * SWA_DIM + Q_LORA + KV_LORA) // SWA_DIM
    vs_col = ks_col + SWA_KV_HEADS
    kr_col = vs_col + SWA_KV_HEADS

    wq = w_uq.reshape(Q_LORA, MLA_HEADS, QK_NOPE + QK_ROPE) * ((QK_NOPE + QK_ROPE) ** -0.5)
    wq2 = jnp.concatenate([wq, _rope_swap_cols(wq[..., QK_NOPE:])], axis=-1)
    wq2 = wq2.reshape(Q_LORA, MLA_HEADS * (QK_NOPE + 2 * QK_ROPE)).astype(BF16)
    qa = norm_proj(proj, cq_col, Q_LORA, q_norm, wq2, name="mla_q_up")
    kv = norm_proj(proj, ckv_col, KV_LORA, kv_norm, w_ukv.astype(BF16), name="mla_kv_up")

    o_a = mla_attention(qa, kv, proj, kr_col, cs, b, l)
    o_b = swa_attention(proj, 0, ks_col, vs_col, sinks, slopes, b, l)
    na = MLA_HEADS * V_DIM
    h = out_proj([o_a, o_b], [w_o[:na].astype(BF16), w_o[na:].astype(BF16)], h, name="even_out_proj")
    return dense_ffn(h, norm2, w_gate_up.astype(BF16), w_down.astype(BF16))


def moe_layer(h, norm2, router, w_gate, w_up, w_down):
    n, d = h.shape
    ne = router.shape[1]
    router_p = jnp.zeros((d, LANES), F32).at[:, :ne].set(router)
    xn, logits = norm_router(h, norm2, router_p)
    top_val, top_idx = lax.top_k(logits[:, :ne], TOP_K)
    gates = jax.nn.softmax(top_val, axis=-1)
    a = n * TOP_K
    tm = MOE_ROWS if a >= 8 * MOE_ROWS else 64
    flat_e = top_idx.reshape(-1)
    flat_tok = jnp.arange(a) // TOP_K
    order = jnp.argsort(flat_e)
    se, st, sg = flat_e[order], flat_tok[order], gates.reshape(-1)[order]
    counts = jnp.bincount(flat_e, length=ne)
    padded = (counts + tm - 1) // tm * tm
    start_sorted = jnp.cumsum(counts) - counts
    end_pad = jnp.cumsum(padded)
    start_pad = end_pad - padded
    dest = start_pad[se] + jnp.arange(a) - start_sorted[se]
    n_blocks = -(-a // tm) + ne
    p = n_blocks * tm
    row_tok = jnp.full((p,), n, dtype=jnp.int32).at[dest].set(st.astype(jnp.int32))
    row_gate = jnp.zeros((p,), F32).at[dest].set(sg)
    blk_start = jnp.arange(n_blocks) * tm
    blk_e = jnp.minimum(jnp.sum(blk_start[:, None] >= end_pad[None, :], axis=1), ne - 1)
    n_used = (end_pad[-1] // tm).astype(jnp.int32).reshape(1)
    x_pad = jnp.concatenate([xn, jnp.zeros((1, d), xn.dtype)], axis=0)
    y = moe_ffn(x_pad[row_tok], blk_e.astype(jnp.int32), n_used,
                w_gate.astype(BF16), w_up.astype(BF16), w_down.astype(BF16), tm)
    y = y * row_gate[:, None]
    out = jnp.zeros((n + 1, d), F32).at[row_tok].add(y)[:n]
    return h + out


def odd_layer(h, b, l, norm1, w_in, b_f, w_o, norm2, router, w_gate, w_up, w_down):
    d = h.shape[1]
    hd = FOX_HEADS * FOX_DIM
    nb = (l - N_META) // BLOCK
    w_qkv = jnp.concatenate([w_in[:, :hd] * (FOX_DIM ** -0.5), w_in[:, hd:3 * hd]], axis=1).astype(BF16)
    w_f = jnp.zeros((d, LANES), F32).at[:, :FOX_HEADS].set(w_in[:, 3 * hd:]).astype(BF16)
    proj, gate_logit = norm_proj(h, 0, d, norm1, w_qkv, w_aux=w_f, name="fox_in_proj")
    log_f = jax.nn.log_sigmoid(gate_logit[:, :FOX_HEADS].reshape(b, l, FOX_HEADS) + b_f.astype(F32))
    c = jnp.swapaxes(jnp.cumsum(log_f, axis=1), 1, 2)
    c_meta = jnp.zeros((b, FOX_HEADS, 1, BLOCK), F32).at[:, :, 0, :N_META].set(c[:, :, :N_META])
    c_real = c[:, :, N_META:].reshape(b, FOX_HEADS, nb, BLOCK)
    o = fox_attention(proj, c_meta, c_real, b, l)
    h = out_proj([o], [w_o.astype(BF16)], h, name="fox_out_proj")
    return moe_layer(h, norm2, router, w_gate, w_up, w_down)


def alibi_slopes(n):
    return jnp.asarray([2.0 ** (-8.0 * (i + 1) / n) for i in range(n)], dtype=F32)


def kernel(x, meta, ev_norm1, ev_w_in, ev_q_norm, ev_kv_norm, ev_w_uq, ev_w_ukv, ev_sinks, ev_w_o,
           ev_norm2, ev_w_gate_up, ev_w_down, od_norm1, od_w_in, od_b_f, od_w_o, od_norm2,
           od_router, od_w_gate, od_w_up, od_w_down, final_norm):
    b, seq, d = x.shape
    l = seq + N_META
    depth = ev_norm1.shape[0] + od_norm1.shape[0]
    h = jnp.concatenate([jnp.broadcast_to(meta[None].astype(x.dtype), (b, N_META, d)), x], axis=1)
    h = h.reshape(b * l, d)
    cs = _rope_table(l)
    slopes = alibi_slopes(SWA_HEADS)
    for layer in range(depth):
        i = layer // 2
        if layer % 2 == 0:
            h = even_layer(h, b, l, ev_norm1[i], ev_w_in[i], ev_q_norm[i], ev_kv_norm[i], ev_w_uq[i],
                           ev_w_ukv[i], ev_sinks[i], ev_w_o[i], ev_norm2[i], ev_w_gate_up[i],
                           ev_w_down[i], cs, slopes)
        else:
            h = odd_layer(h, b, l, od_norm1[i], od_w_in[i], od_b_f[i], od_w_o[i], od_norm2[i],
                          od_router[i], od_w_gate[i], od_w_up[i], od_w_down[i])
    out = final_rmsnorm(h, final_norm)
    return out.reshape(b, l, d)[:, N_META:]
```

```python
import functools

import jax
import jax.numpy as jnp
from jax import lax
from jax.experimental import pallas as pl
from jax.experimental.pallas import tpu as pltpu

F32 = jnp.float32
BF16 = jnp.bfloat16

N_META = 16
BLOCK = 128
EPS = 1e-6
MLA_HEADS = 8
Q_LORA = 512
KV_LORA = 256
QK_NOPE = 128
QK_ROPE = 64
V_DIM = 128
ROPE_THETA = 10000.0
SWA_HEADS = 8
SWA_KV_HEADS = 2
SWA_DIM = 128
SWA_GROUP = SWA_HEADS // SWA_KV_HEADS
FOX_HEADS = 16
FOX_DIM = 128
TOP_K = 2
MOE_ROWS = 512
LANES = 128
NEG = -1e30
V7X_VMEM_BYTES = 64 * 1024 * 1024
VMEM_LIMIT = V7X_VMEM_BYTES - 8 * 1024 * 1024


def _params(*sem):
    return pltpu.CompilerParams(dimension_semantics=sem, vmem_limit_bytes=VMEM_LIMIT)


def _pick(n, candidates):
    for c in candidates:
        if n % c == 0:
            return c
    return n


def _row_tile(n):
    return _pick(n, (688, 512, 384, 256, 128, 64, 32, 16))


def _norm_rows(x, g):
    ms = jnp.mean(x * x, axis=-1, keepdims=True)
    return (x * lax.rsqrt(ms + EPS)) * g


def _norm_proj_kernel(x_ref, g_ref, w_ref, *rest, has_aux):
    if has_aux:
        waux_ref, o_ref, oaux_ref, xn_ref = rest
    else:
        o_ref, xn_ref = rest

    @pl.when(pl.program_id(1) == 0)
    def _():
        xn = _norm_rows(x_ref[...].astype(F32), g_ref[...]).astype(BF16)
        xn_ref[...] = xn
        if has_aux:
            oaux_ref[...] = jnp.dot(xn, waux_ref[...], preferred_element_type=F32)

    o_ref[...] = jnp.dot(xn_ref[...], w_ref[...], preferred_element_type=F32).astype(o_ref.dtype)


def norm_proj(x, xcol, k, gain, w, w_aux=None, out_dtype=BF16, name="norm_proj"):
    n = x.shape[0]
    nout = w.shape[1]
    tm = _row_tile(n)
    tn = _pick(nout, (1024, 512, 256, 128))
    has_aux = w_aux is not None
    in_specs = [
        pl.BlockSpec((tm, k), lambda i, j: (i, xcol)),
        pl.BlockSpec((1, k), lambda i, j: (0, 0)),
        pl.BlockSpec((k, tn), lambda i, j: (0, j)),
    ]
    args = [x, gain.reshape(1, k).astype(F32), w]
    out_shape = [jax.ShapeDtypeStruct((n, nout), out_dtype)]
    out_specs = [pl.BlockSpec((tm, tn), lambda i, j: (i, j))]
    if has_aux:
        na = w_aux.shape[1]
        in_specs.append(pl.BlockSpec((k, na), lambda i, j: (0, 0)))
        args.append(w_aux)
        out_shape.append(jax.ShapeDtypeStruct((n, na), F32))
        out_specs.append(pl.BlockSpec((tm, na), lambda i, j: (i, 0)))
    res = pl.pallas_call(
        functools.partial(_norm_proj_kernel, has_aux=has_aux),
        grid=(n // tm, nout // tn),
        in_specs=in_specs,
        out_specs=out_specs,
        out_shape=out_shape,
        scratch_shapes=[pltpu.VMEM((tm, k), BF16)],
        compiler_params=_params("parallel", "arbitrary"),
        name=name,
    )(*args)
    return res if has_aux else res[0]


def _out_proj_kernel(*refs, n_lhs):
    a_refs = refs[:n_lhs]
    w_refs = refs[n_lhs:2 * n_lhs]
    res_ref, o_ref = refs[2 * n_lhs:]
    acc = res_ref[...]
    for a_ref, w_ref in zip(a_refs, w_refs):
        acc = acc + jnp.dot(a_ref[...], w_ref[...], preferred_element_type=F32)
    o_ref[...] = acc


def out_proj(lhs, ws, res, name="out_proj"):
    n, d = res.shape
    tm = _row_tile(n)
    tn = _pick(d, (1024, 512, 256, 128))
    in_specs = []
    for a in lhs:
        in_specs.append(pl.BlockSpec((tm, a.shape[1]), lambda i, j: (i, 0)))
    for w in ws:
        in_specs.append(pl.BlockSpec((w.shape[0], tn), lambda i, j: (0, j)))
    in_specs.append(pl.BlockSpec((tm, tn), lambda i, j: (i, j)))
    return pl.pallas_call(
        functools.partial(_out_proj_kernel, n_lhs=len(lhs)),
        grid=(n // tm, d // tn),
        in_specs=in_specs,
        out_specs=pl.BlockSpec((tm, tn), lambda i, j: (i, j)),
        out_shape=jax.ShapeDtypeStruct((n, d), F32),
        compiler_params=_params("parallel", "parallel"),
        name=name,
    )(*lhs, *ws, res)


def _norm_router_kernel(x_ref, g_ref, wr_ref, xn_ref, logit_ref):
    xn = _norm_rows(x_ref[...], g_ref[...])
    xn_ref[...] = xn.astype(BF16)
    logit_ref[...] = jnp.dot(xn, wr_ref[...], preferred_element_type=F32,
                             precision=lax.Precision.HIGHEST)


def norm_router(h, gain, router_padded):
    n, d = h.shape
    tm = _row_tile(n)
    ne = router_padded.shape[1]
    return pl.pallas_call(
        _norm_router_kernel,
        grid=(n // tm,),
        in_specs=[pl.BlockSpec((tm, d), lambda i: (i, 0)),
                  pl.BlockSpec((1, d), lambda i: (0, 0)),
                  pl.BlockSpec((d, ne), lambda i: (0, 0))],
        out_specs=[pl.BlockSpec((tm, d), lambda i: (i, 0)),
                   pl.BlockSpec((tm, ne), lambda i: (i, 0))],
        out_shape=[jax.ShapeDtypeStruct((n, d), BF16), jax.ShapeDtypeStruct((n, ne), F32)],
        compiler_params=_params("parallel"),
        name="norm_router",
    )(h, gain.reshape(1, d).astype(F32), router_padded)


def _final_norm_kernel(x_ref, g_ref, o_ref):
    o_ref[...] = _norm_rows(x_ref[...], g_ref[...])


def final_rmsnorm(h, gain):
    n, d = h.shape
    tm = _row_tile(n)
    return pl.pallas_call(
        _final_norm_kernel,
        grid=(n // tm,),
        in_specs=[pl.BlockSpec((tm, d), lambda i: (i, 0)), pl.BlockSpec((1, d), lambda i: (0, 0))],
        out_specs=pl.BlockSpec((tm, d), lambda i: (i, 0)),
        out_shape=jax.ShapeDtypeStruct((n, d), F32),
        compiler_params=_params("parallel"),
        name="final_norm",
    )(h, gain.reshape(1, d).astype(F32))


def _swiglu_part(xn, wg, wu, wd):
    g = jnp.dot(xn, wg, preferred_element_type=F32)
    u = jnp.dot(xn, wu, preferred_element_type=F32)
    a = (g * jax.nn.sigmoid(g) * u).astype(BF16)
    return jnp.dot(a, wd, preferred_element_type=F32)


def _dense_ffn_kernel(x_ref, g_ref, wg_ref, wu_ref, wd_ref, o_ref, xn_ref):
    j = pl.program_id(1)

    @pl.when(j == 0)
    def _():
        xn_ref[...] = _norm_rows(x_ref[...], g_ref[...]).astype(BF16)

    part = _swiglu_part(xn_ref[...], wg_ref[...], wu_ref[...], wd_ref[...])

    @pl.when(j == 0)
    def _():
        o_ref[...] = x_ref[...] + part

    @pl.when(j > 0)
    def _():
        o_ref[...] += part


def dense_ffn(h, gain, w_gate_up, w_down, layer):
    n, d = h.shape
    dff = w_down.shape[1]
    tm = _row_tile(n)
    tf = _pick(dff, (512, 256, 128))
    nj = dff // tf
    return pl.pallas_call(
        _dense_ffn_kernel,
        grid=(n // tm, nj),
        in_specs=[pl.BlockSpec((tm, d), lambda i, j: (i, 0)),
                  pl.BlockSpec((1, d), lambda i, j: (0, 0)),
                  pl.BlockSpec((None, d, tf), lambda i, j: (layer, 0, j)),
                  pl.BlockSpec((None, d, tf), lambda i, j: (layer, 0, nj + j)),
                  pl.BlockSpec((None, tf, d), lambda i, j: (layer, j, 0))],
        out_specs=pl.BlockSpec((tm, d), lambda i, j: (i, 0)),
        out_shape=jax.ShapeDtypeStruct((n, d), F32),
        scratch_shapes=[pltpu.VMEM((tm, d), BF16)],
        compiler_params=_params("parallel", "arbitrary"),
        name="dense_ffn",
    )(h, gain.reshape(1, d).astype(F32), w_gate_up, w_gate_up, w_down)


def _moe_ffn_kernel(be_ref, nused_ref, x_ref, wg_ref, wu_ref, wd_ref, o_ref):
    b = pl.program_id(0)
    j = pl.program_id(1)
    used = b < nused_ref[0]

    @pl.when(used)
    def _():
        part = _swiglu_part(x_ref[...], wg_ref[0], wu_ref[0], wd_ref[0])

        @pl.when(j == 0)
        def _():
            o_ref[...] = part

        @pl.when(j > 0)
        def _():
            o_ref[...] += part

    @pl.when(jnp.logical_and(jnp.logical_not(used), j == 0))
    def _():
        o_ref[...] = jnp.zeros_like(o_ref)


def moe_ffn(x_sorted, blk_e, n_used, w_gate, w_up, w_down, tm):
    p, d = x_sorted.shape
    dff = w_down.shape[1]
    tf = _pick(dff, (512, 256, 128))
    nj = dff // tf
    nb = p // tm

    def jj(b, j, nused):
        return jnp.where(b < nused[0], j, nj - 1)

    grid_spec = pltpu.PrefetchScalarGridSpec(
        num_scalar_prefetch=2,
        grid=(nb, nj),
        in_specs=[pl.BlockSpec((tm, d), lambda b, j, be, nu: (b, 0)),
                  pl.BlockSpec((1, d, tf), lambda b, j, be, nu: (be[b], 0, jj(b, j, nu))),
                  pl.BlockSpec((1, d, tf), lambda b, j, be, nu: (be[b], 0, jj(b, j, nu))),
                  pl.BlockSpec((1, tf, d), lambda b, j, be, nu: (be[b], jj(b, j, nu), 0))],
        out_specs=pl.BlockSpec((tm, d), lambda b, j, be, nu: (b, 0)),
    )
    return pl.pallas_call(
        _moe_ffn_kernel,
        grid_spec=grid_spec,
        out_shape=jax.ShapeDtypeStruct((p, d), F32),
        compiler_params=_params("parallel", "arbitrary"),
        name="moe_ffn",
    )(blk_e, n_used, x_sorted, w_gate, w_up, w_down)


LOG2E = 1.4426950408889634


def _qk(q, k):
    return lax.dot_general(q, k, (((1,), (1,)), ((), ())), preferred_element_type=F32)


def _pv(p, v):
    return jnp.dot(p.astype(BF16), v, preferred_element_type=F32)


def _flash_update(s, v, m, l, acc):
    m_new = jnp.maximum(m, jnp.max(s, axis=-1, keepdims=True))
    alpha = jnp.exp2(m - m_new)
    p = jnp.exp2(s - m_new)
    l = alpha * l + jnp.sum(p, axis=-1, keepdims=True)
    acc = alpha * acc + _pv(p, v)
    return m_new, l, acc


def _iota2(shape):
    return (lax.broadcasted_iota(jnp.int32, shape, 0), lax.broadcasted_iota(jnp.int32, shape, 1))


def _query_tile(seq):
    return _pick(seq, (512, 256, 128))


def _causal_sweep(q_ref, k_ref, v_ref, o_ref, col_bias, nq, tq):
    r16, c16 = _iota2((N_META, BLOCK))
    k0 = k_ref[0:BLOCK, :]
    v0 = v_ref[0:BLOCK, :]

    s = _qk(q_ref[0:N_META, :], k0) + col_bias(-1)
    s = jnp.where(c16 <= r16, s, NEG)
    p = jnp.exp2(s - jnp.max(s, axis=-1, keepdims=True))
    o = _pv(p, v0) * (1.0 / jnp.sum(p, axis=-1, keepdims=True))
    o_ref[0:N_META, :] = o.astype(o_ref.dtype)

    def qblock(i, carry):
        qs = pl.multiple_of(N_META + i * tq, N_META)
        q = q_ref[pl.ds(qs, tq), :]
        cm = lax.broadcasted_iota(jnp.int32, (tq, BLOCK), 1)
        s = jnp.where(cm < N_META, _qk(q, k0) + col_bias(-1), NEG)
        m = jnp.max(s, axis=-1, keepdims=True)
        p = jnp.exp2(s - m)
        l = jnp.sum(p, axis=-1, keepdims=True)
        acc = _pv(p, v0)

        def chunk(t, mla):
            ks = pl.multiple_of(N_META + t * tq, N_META)
            s = _qk(q, k_ref[pl.ds(ks, tq), :]) + col_bias(t)
            return _flash_update(s, v_ref[pl.ds(ks, tq), :], *mla)

        m, l, acc = lax.fori_loop(0, i, chunk, (m, l, acc))
        r, c = _iota2((tq, tq))
        s = _qk(q, k_ref[pl.ds(qs, tq), :]) + col_bias(i)
        s = jnp.where(c <= r, s, NEG)
        m, l, acc = _flash_update(s, v_ref[pl.ds(qs, tq), :], m, l, acc)
        o_ref[pl.ds(qs, tq), :] = (acc * (1.0 / l)).astype(o_ref.dtype)
        return carry

    lax.fori_loop(0, nq, qblock, 0)


def _fox_kernel(q_ref, k_ref, v_ref, cm_ref, cr_ref, o_ref, *, nq, tq):
    def col_bias(t):
        if isinstance(t, int):
            return -cm_ref[...]
        return -cr_ref[pl.ds(t, 1), :]

    _causal_sweep(q_ref, k_ref, v_ref, o_ref, col_bias, nq, tq)


def fox_attention(proj, c_meta, c_real, b, l):
    nq, tq = c_real.shape[2:]
    h = FOX_HEADS
    p3 = proj.reshape(b, l, 3 * h * FOX_DIM)
    blk = lambda off: pl.BlockSpec((None, l, FOX_DIM), lambda bi, hi: (bi, 0, off + hi))
    out = pl.pallas_call(
        functools.partial(_fox_kernel, nq=nq, tq=tq),
        grid=(b, h),
        in_specs=[blk(0), blk(h), blk(2 * h),
                  pl.BlockSpec((None, None, 1, BLOCK), lambda bi, hi: (bi, hi, 0, 0)),
                  pl.BlockSpec((None, None, nq, tq), lambda bi, hi: (bi, hi, 0, 0))],
        out_specs=pl.BlockSpec((None, l, FOX_DIM), lambda bi, hi: (bi, 0, hi)),
        out_shape=jax.ShapeDtypeStruct((b, l, h * FOX_DIM), BF16),
        compiler_params=_params("parallel", "parallel"),
        name="fox_attention",
    )(p3, p3, p3, c_meta, c_real)
    return out.reshape(b * l, h * FOX_DIM)


def _mla_kernel(q_ref, kv_ref, kr_ref, cs_ref, o_ref, qx_ref, kx_ref, *, nq, tq):
    cs = cs_ref[...]
    lane = lax.broadcasted_iota(jnp.int32, cs.shape, 1)
    eq = q_ref[:, QK_NOPE:].astype(F32) * cs
    qx_ref[:, :QK_NOPE] = q_ref[:, :QK_NOPE]
    qx_ref[:, QK_NOPE:] = (eq + pltpu.roll(eq, QK_ROPE, axis=1)).astype(BF16)
    ek = kr_ref[...].astype(F32) * cs
    rk = jnp.where(lane < QK_ROPE, ek + pltpu.roll(ek, QK_ROPE, axis=1), 0.0)
    kx_ref[:, :QK_NOPE] = kv_ref[:, :QK_NOPE]
    kx_ref[:, QK_NOPE:] = rk.astype(BF16)
    v_ref = kv_ref.at[:, QK_NOPE:]
    _causal_sweep(qx_ref, kx_ref, v_ref, o_ref, lambda t: 0.0, nq, tq)


def mla_attention(qa, kv, proj, kr_col, cs, b, l):
    tq = _query_tile(l - N_META)
    nq = (l - N_META) // tq
    h = MLA_HEADS
    w = QK_NOPE + 2 * QK_ROPE
    qa3 = qa.reshape(b, l, h * w)
    kv3 = kv.reshape(b, l, h * (QK_NOPE + V_DIM))
    p3 = proj.reshape(b, l, proj.shape[1])
    out = pl.pallas_call(
        functools.partial(_mla_kernel, nq=nq, tq=tq),
        grid=(b, h),
        in_specs=[pl.BlockSpec((None, l, w), lambda bi, hi: (bi, 0, hi)),
                  pl.BlockSpec((None, l, QK_NOPE + V_DIM), lambda bi, hi: (bi, 0, hi)),
                  pl.BlockSpec((None, l, 2 * QK_ROPE), lambda bi, hi: (bi, 0, kr_col)),
                  pl.BlockSpec((l, 2 * QK_ROPE), lambda bi, hi: (0, 0))],
        out_specs=pl.BlockSpec((None, l, V_DIM), lambda bi, hi: (bi, 0, hi)),
        out_shape=jax.ShapeDtypeStruct((b, l, h * V_DIM), BF16),
        scratch_shapes=[pltpu.VMEM((l, w), BF16), pltpu.VMEM((l, w), BF16)],
        compiler_params=_params("parallel", "parallel"),
        name="mla_attention",
    )(qa3, kv3, p3, cs)
    return out.reshape(b * l, h * V_DIM)


def _swa_kernel(sink_ref, slope_ref, q_ref, k_ref, v_ref, o_ref, *, nb):
    kvh = pl.program_id(1)
    r16, c16 = _iota2((N_META, BLOCK))
    r, c = _iota2((BLOCK, BLOCK))
    rel = (r - c).astype(F32)
    rel16 = (r16 - c16).astype(F32)
    k0 = k_ref[0:BLOCK, :]
    v0 = v_ref[0:BLOCK, :]
    sinks = [sink_ref[kvh * SWA_GROUP + g] for g in range(SWA_GROUP)]
    slopes = [slope_ref[kvh * SWA_GROUP + g] for g in range(SWA_GROUP)]

    for g in range(SWA_GROUP):
        cols = slice(g * SWA_DIM, (g + 1) * SWA_DIM)
        s = _qk(q_ref[0:N_META, cols], k0) - slopes[g] * rel16
        s = jnp.where(c16 <= r16, s, NEG)
        m = jnp.maximum(jnp.max(s, axis=-1, keepdims=True), sinks[g])
        p = jnp.exp2(s - m)
        l = jnp.sum(p, axis=-1, keepdims=True) + jnp.exp2(sinks[g] - m)
        o_ref[0:N_META, cols] = (_pv(p, v0) * (1.0 / l)).astype(o_ref.dtype)

    def qblock(i, carry):
        qs = pl.multiple_of(N_META + i * BLOCK, N_META)
        ps = pl.multiple_of(jnp.where(i > 0, qs - BLOCK, qs), N_META)
        kp = k_ref[pl.ds(ps, BLOCK), :]
        vp = v_ref[pl.ds(ps, BLOCK), :]
        kc = k_ref[pl.ds(qs, BLOCK), :]
        vc = v_ref[pl.ds(qs, BLOCK), :]
        dist_m = rel + (N_META + i * BLOCK).astype(F32)
        dist_p = rel + float(BLOCK)
        prev_ok = jnp.logical_and(c > r, i > 0)
        for g in range(SWA_GROUP):
            cols = slice(g * SWA_DIM, (g + 1) * SWA_DIM)
            q = q_ref[pl.ds(qs, BLOCK), cols]
            s_m = jnp.where(c < N_META, _qk(q, k0) - slopes[g] * dist_m, NEG)
            s_p = jnp.where(prev_ok, _qk(q, kp) - slopes[g] * dist_p, NEG)
            s_c = jnp.where(c <= r, _qk(q, kc) - slopes[g] * rel, NEG)
            m = jnp.max(jnp.maximum(jnp.maximum(s_m, s_p), s_c), axis=-1, keepdims=True)
            m = jnp.maximum(m, sinks[g])
            p_m, p_p, p_c = jnp.exp2(s_m - m), jnp.exp2(s_p - m), jnp.exp2(s_c - m)
            l = jnp.sum(p_m + p_p + p_c, axis=-1, keepdims=True) + jnp.exp2(sinks[g] - m)
            o = _pv(p_m, v0) + _pv(p_p, vp) + _pv(p_c, vc)
            o_ref[pl.ds(qs, BLOCK), cols] = (o * (1.0 / l)).astype(o_ref.dtype)
        return carry

    lax.fori_loop(0, nb, qblock, 0)


def swa_attention(proj, q_col, k_col, v_col, sinks, slopes, b, l):
    nb = (l - N_META) // BLOCK
    gw = SWA_GROUP * SWA_DIM
    p3 = proj.reshape(b, l, proj.shape[1])
    grid_spec = pltpu.PrefetchScalarGridSpec(
        num_scalar_prefetch=2,
        grid=(b, SWA_KV_HEADS),
        in_specs=[pl.BlockSpec((None, l, gw), lambda bi, hi, s0, s1: (bi, 0, q_col + hi)),
                  pl.BlockSpec((None, l, SWA_DIM), lambda bi, hi, s0, s1: (bi, 0, k_col + hi)),
                  pl.BlockSpec((None, l, SWA_DIM), lambda bi, hi, s0, s1: (bi, 0, v_col + hi))],
        out_specs=pl.BlockSpec((None, l, gw), lambda bi, hi, s0, s1: (bi, 0, hi)),
    )
    out = pl.pallas_call(
        functools.partial(_swa_kernel, nb=nb),
        grid_spec=grid_spec,
        out_shape=jax.ShapeDtypeStruct((b, l, SWA_HEADS * SWA_DIM), BF16),
        compiler_params=_params("parallel", "parallel"),
        name="swa_attention",
    )(sinks, slopes, p3, p3, p3)
    return out.reshape(b * l, SWA_HEADS * SWA_DIM)


def _rope_swap_cols(w):
    half = QK_ROPE // 2
    return jnp.concatenate([-w[..., half:], w[..., :half]], axis=-1)


def _rope_table(l):
    half = QK_ROPE // 2
    inv = ROPE_THETA ** (-jnp.arange(half, dtype=F32) / half)
    ang = jnp.arange(l, dtype=F32)[:, None] * inv[None, :]
    cos, sin = jnp.cos(ang), jnp.sin(ang)
    return jnp.concatenate([cos, cos, sin, sin], axis=-1)


def even_layer(h, b, l, layer, norm1, w_in, q_norm, kv_norm, w_uq, w_ukv, sinks, w_o, norm2,
               w_gate_up, w_down, cs, slopes):
    d = h.shape[1]
    o1, o2, o3 = Q_LORA, Q_LORA + KV_LORA, Q_LORA + KV_LORA + QK_ROPE
    o4 = o3 + SWA_HEADS * SWA_DIM
    o5 = o4 + SWA_KV_HEADS * SWA_DIM
    w_kr = w_in[:, o2:o3]
    w_in2 = jnp.concatenate([w_in[:, o3:o4] * (SWA_DIM ** -0.5 * LOG2E), w_in[:, :o1], w_in[:, o1:o2],
                             w_in[:, o4:o5], w_in[:, o5:], w_kr, _rope_swap_cols(w_kr)],
                            axis=1).astype(BF16)
    w_in2 = jnp.pad(w_in2, ((0, 0), (0, -w_in2.shape[1] % 512)))
    proj = norm_proj(h, 0, d, norm1, w_in2, name="even_in_proj")
    cq_col = (SWA_HEADS * SWA_DIM) // Q_LORA
    ckv_col = (SWA_HEADS * SWA_DIM + Q_LORA) // KV_LORA
    ks_col = (SWA_HEADS * SWA_DIM + Q_LORA + KV_LORA) // SWA_DIM
    vs_col = ks_col + SWA_KV_HEADS
    kr_col = vs_col + SWA_KV_HEADS

    wq = w_uq.reshape(Q_LORA, MLA_HEADS, QK_NOPE + QK_ROPE) * ((QK_NOPE + QK_ROPE) ** -0.5 * LOG2E)
    wq2 = jnp.concatenate([wq, _rope_swap_cols(wq[..., QK_NOPE:])], axis=-1)
    wq2 = wq2.reshape(Q_LORA, MLA_HEADS * (QK_NOPE + 2 * QK_ROPE)).astype(BF16)
    qa = norm_proj(proj, cq_col, Q_LORA, q_norm, wq2, name="mla_q_up")
    kv = norm_proj(proj, ckv_col, KV_LORA, kv_norm, w_ukv.astype(BF16), name="mla_kv_up")

    o_a = mla_attention(qa, kv, proj, kr_col, cs, b, l)
    o_b = swa_attention(proj, 0, ks_col, vs_col, sinks.astype(F32) * LOG2E, slopes * LOG2E, b, l)
    na = MLA_HEADS * V_DIM
    h = out_proj([o_a, o_b], [w_o[:na].astype(BF16), w_o[na:].astype(BF16)], h, name="even_out_proj")
    return dense_ffn(h, norm2, w_gate_up, w_down, layer)


def moe_layer(h, layer, norm2, router, w_gate, w_up, w_down):
    n, d = h.shape
    ne = router.shape[1]
    router_p = jnp.zeros((d, LANES), F32).at[:, :ne].set(router)
    xn, logits = norm_router(h, norm2, router_p)
    top_val, top_idx = lax.top_k(logits[:, :ne], TOP_K)
    gates = jax.nn.softmax(top_val, axis=-1)
    a = n * TOP_K
    tm = MOE_ROWS if a >= 8 * MOE_ROWS else 64
    flat_e = top_idx.reshape(-1)
    flat_tok = jnp.arange(a) // TOP_K
    order = jnp.argsort(flat_e)
    se, st, sg = flat_e[order], flat_tok[order], gates.reshape(-1)[order]
    counts = jnp.bincount(flat_e, length=ne)
    padded = (counts + tm - 1) // tm * tm
    start_sorted = jnp.cumsum(counts) - counts
    end_pad = jnp.cumsum(padded)
    start_pad = end_pad - padded
    dest = start_pad[se] + jnp.arange(a) - start_sorted[se]
    n_blocks = -(-a // tm) + ne
    p = n_blocks * tm
    row_tok = jnp.full((p,), n, dtype=jnp.int32).at[dest].set(st.astype(jnp.int32))
    row_gate = jnp.zeros((p,), F32).at[dest].set(sg)
    blk_start = jnp.arange(n_blocks) * tm
    blk_e = jnp.minimum(jnp.sum(blk_start[:, None] >= end_pad[None, :], axis=1), ne - 1)
    n_used = (end_pad[-1] // tm).astype(jnp.int32).reshape(1)
    x_pad = jnp.concatenate([xn, jnp.zeros((1, d), xn.dtype)], axis=0)
    y = moe_ffn(x_pad[row_tok], (blk_e + layer * ne).astype(jnp.int32), n_used, w_gate, w_up, w_down, tm)
    y = y * row_gate[:, None]
    out = jnp.zeros((n + 1, d), F32).at[row_tok].add(y)[:n]
    return h + out


def odd_layer(h, b, l, layer, norm1, w_in, b_f, w_o, norm2, router, w_gate, w_up, w_down):
    d = h.shape[1]
    hd = FOX_HEADS * FOX_DIM
    tq = _query_tile(l - N_META)
    w_qkv = jnp.concatenate([w_in[:, :hd] * (FOX_DIM ** -0.5 * LOG2E), w_in[:, hd:3 * hd]],
                            axis=1).astype(BF16)
    w_f = jnp.zeros((d, LANES), F32).at[:, :FOX_HEADS].set(w_in[:, 3 * hd:]).astype(BF16)
    proj, gate_logit = norm_proj(h, 0, d, norm1, w_qkv, w_aux=w_f, name="fox_in_proj")
    log_f = jax.nn.log_sigmoid(gate_logit[:, :FOX_HEADS].reshape(b, l, FOX_HEADS) + b_f.astype(F32))
    c = jnp.swapaxes(jnp.cumsum(log_f, axis=1), 1, 2) * LOG2E
    c_meta = jnp.zeros((b, FOX_HEADS, 1, BLOCK), F32).at[:, :, 0, :N_META].set(c[:, :, :N_META])
    c_real = c[:, :, N_META:].reshape(b, FOX_HEADS, (l - N_META) // tq, tq)
    o = fox_attention(proj, c_meta, c_real, b, l)
    h = out_proj([o], [w_o.astype(BF16)], h, name="fox_out_proj")
    return moe_layer(h, layer, norm2, router, w_gate, w_up, w_down)


def alibi_slopes(n):
    return jnp.asarray([2.0 ** (-8.0 * (i + 1) / n) for i in range(n)], dtype=F32)


def kernel(x, meta, ev_norm1, ev_w_in, ev_q_norm, ev_kv_norm, ev_w_uq, ev_w_ukv, ev_sinks, ev_w_o,
           ev_norm2, ev_w_gate_up, ev_w_down, od_norm1, od_w_in, od_b_f, od_w_o, od_norm2,
           od_router, od_w_gate, od_w_up, od_w_down, final_norm):
    b, seq, d = x.shape
    l = seq + N_META
    depth = ev_norm1.shape[0] + od_norm1.shape[0]
    h = jnp.concatenate([jnp.broadcast_to(meta[None].astype(x.dtype), (b, N_META, d)), x], axis=1)
    h = h.reshape(b * l, d)
    cs = _rope_table(l)
    slopes = alibi_slopes(SWA_HEADS)
    ev_gu, ev_dn = ev_w_gate_up.astype(BF16), ev_w_down.astype(BF16)
    dff = od_w_gate.shape[-1]
    od_g = od_w_gate.astype(BF16).reshape(-1, d, dff)
    od_u = od_w_up.astype(BF16).reshape(-1, d, dff)
    od_d = od_w_down.astype(BF16).reshape(-1, dff, d)
    for layer in range(depth):
        i = layer // 2
        if layer % 2 == 0:
            h = even_layer(h, b, l, i, ev_norm1[i], ev_w_in[i], ev_q_norm[i], ev_kv_norm[i], ev_w_uq[i],
                           ev_w_ukv[i], ev_sinks[i], ev_w_o[i], ev_norm2[i], ev_gu, ev_dn, cs, slopes)
        else:
            h = odd_layer(h, b, l, i, od_norm1[i], od_w_in[i], od_b_f[i], od_w_o[i], od_norm2[i],
                          od_router[i], od_g, od_u, od_d)
    out = final_rmsnorm(h, final_norm)
    return out.reshape(b, l, d)[:, N_META:]
```

```python
import functools

import jax
import jax.numpy as jnp
from jax import lax
from jax.experimental import pallas as pl
from jax.experimental.pallas import tpu as pltpu

F32 = jnp.float32
BF16 = jnp.bfloat16

N_META = 16
BLOCK = 128
EPS = 1e-6
MLA_HEADS = 8
Q_LORA = 512
KV_LORA = 256
QK_NOPE = 128
QK_ROPE = 64
V_DIM = 128
ROPE_THETA = 10000.0
SWA_HEADS = 8
SWA_KV_HEADS = 2
SWA_DIM = 128
SWA_GROUP = SWA_HEADS // SWA_KV_HEADS
FOX_HEADS = 16
FOX_DIM = 128
TOP_K = 2
MOE_ROWS = 512
LANES = 128
NEG = -1e30
LOG2E = 1.4426950408889634
V7X_VMEM_BYTES = 64 * 1024 * 1024
VMEM_LIMIT = V7X_VMEM_BYTES - 8 * 1024 * 1024


def _params(*sem):
    return pltpu.CompilerParams(dimension_semantics=sem, vmem_limit_bytes=VMEM_LIMIT)


def _pick(n, candidates):
    for c in candidates:
        if n % c == 0:
            return c
    return n


def _row_tile(n):
    return _pick(n, (688, 512, 384, 256, 128, 64, 32, 16))


def _norm_rows(x, g):
    ms = jnp.mean(x * x, axis=-1, keepdims=True)
    return (x * lax.rsqrt(ms + EPS)) * g


def _norm_proj_kernel(*refs, has_add, has_aux):
    refs = list(refs)
    x_ref = refs.pop(0)
    y0_ref, y1_ref = (refs.pop(0), refs.pop(0)) if has_add else (None, None)
    g_ref, w_ref = refs.pop(0), refs.pop(0)
    waux_ref = refs.pop(0) if has_aux else None
    o_ref = refs.pop(0)
    xsum_ref = refs.pop(0) if has_add else None
    oaux_ref = refs.pop(0) if has_aux else None
    xn_ref = refs.pop(0)

    @pl.when(pl.program_id(1) == 0)
    def _():
        x = x_ref[...].astype(F32)
        if has_add:
            x = x + y0_ref[...] + y1_ref[...]
            xsum_ref[...] = x
        xn = _norm_rows(x, g_ref[...]).astype(BF16)
        xn_ref[...] = xn
        if has_aux:
            oaux_ref[...] = jnp.dot(xn, waux_ref[...], preferred_element_type=F32)

    o_ref[...] = jnp.dot(xn_ref[...], w_ref[...], preferred_element_type=F32).astype(o_ref.dtype)


def norm_proj(x, xcol, k, gain, w, addends=None, w_aux=None, name="norm_proj"):
    n = x.shape[0]
    nout = w.shape[1]
    has_add = addends is not None
    has_aux = w_aux is not None
    tm = _pick(n, (384, 256, 128, 64, 32, 16)) if has_add else _row_tile(n)
    tn = _pick(nout, (1024, 512, 256, 128))
    in_specs = [pl.BlockSpec((tm, k), lambda i, j: (i, xcol))]
    args = [x]
    if has_add:
        nblk = n // tm
        in_specs += [pl.BlockSpec((tm, k), lambda i, j: (i, 0)),
                     pl.BlockSpec((tm, k), lambda i, j: (nblk + i, 0))]
        args += [addends, addends]
    in_specs += [pl.BlockSpec((1, k), lambda i, j: (0, 0)), pl.BlockSpec((k, tn), lambda i, j: (0, j))]
    args += [gain.reshape(1, k).astype(F32), w]
    if has_aux:
        in_specs.append(pl.BlockSpec((k, w_aux.shape[1]), lambda i, j: (0, 0)))
        args.append(w_aux)
    out_shape = [jax.ShapeDtypeStruct((n, nout), BF16)]
    out_specs = [pl.BlockSpec((tm, tn), lambda i, j: (i, j))]
    if has_add:
        out_shape.append(jax.ShapeDtypeStruct((n, k), F32))
        out_specs.append(pl.BlockSpec((tm, k), lambda i, j: (i, 0)))
    if has_aux:
        out_shape.append(jax.ShapeDtypeStruct((n, w_aux.shape[1]), F32))
        out_specs.append(pl.BlockSpec((tm, w_aux.shape[1]), lambda i, j: (i, 0)))
    return pl.pallas_call(
        functools.partial(_norm_proj_kernel, has_add=has_add, has_aux=has_aux),
        grid=(n // tm, nout // tn),
        in_specs=in_specs,
        out_specs=out_specs,
        out_shape=out_shape,
        scratch_shapes=[pltpu.VMEM((tm, k), BF16)],
        compiler_params=_params("parallel", "arbitrary"),
        name=name,
    )(*args)


def _out_proj_kernel(*refs, n_lhs):
    a_refs = refs[:n_lhs]
    w_refs = refs[n_lhs:2 * n_lhs]
    res_ref, o_ref = refs[2 * n_lhs:]
    acc = res_ref[...]
    for a_ref, w_ref in zip(a_refs, w_refs):
        acc = acc + jnp.dot(a_ref[...], w_ref[...], preferred_element_type=F32)
    o_ref[...] = acc


def out_proj(lhs, ws, res, name="out_proj"):
    n, d = res.shape
    tm = _row_tile(n)
    tn = _pick(d, (1024, 512, 256, 128))
    in_specs = []
    for a in lhs:
        in_specs.append(pl.BlockSpec((tm, a.shape[1]), lambda i, j: (i, 0)))
    for w in ws:
        in_specs.append(pl.BlockSpec((w.shape[0], tn), lambda i, j: (0, j)))
    in_specs.append(pl.BlockSpec((tm, tn), lambda i, j: (i, j)))
    return pl.pallas_call(
        functools.partial(_out_proj_kernel, n_lhs=len(lhs)),
        grid=(n // tm, d // tn),
        in_specs=in_specs,
        out_specs=pl.BlockSpec((tm, tn), lambda i, j: (i, j)),
        out_shape=jax.ShapeDtypeStruct((n, d), F32),
        compiler_params=_params("parallel", "parallel"),
        name=name,
    )(*lhs, *ws, res)


def _router_kernel(x_ref, g_ref, wr_ref, logit_ref):
    xn = _norm_rows(x_ref[...], g_ref[...])
    logit_ref[...] = jnp.dot(xn, wr_ref[...], preferred_element_type=F32,
                             precision=lax.Precision.HIGHEST)


def router_logits(h, gain, router_padded):
    n, d = h.shape
    tm = _row_tile(n)
    ne = router_padded.shape[1]
    return pl.pallas_call(
        _router_kernel,
        grid=(n // tm,),
        in_specs=[pl.BlockSpec((tm, d), lambda i: (i, 0)),
                  pl.BlockSpec((1, d), lambda i: (0, 0)),
                  pl.BlockSpec((d, ne), lambda i: (0, 0))],
        out_specs=pl.BlockSpec((tm, ne), lambda i: (i, 0)),
        out_shape=jax.ShapeDtypeStruct((n, ne), F32),
        compiler_params=_params("parallel"),
        name="router_logits",
    )(h, gain.reshape(1, d).astype(F32), router_padded)


def _final_norm_kernel(x_ref, y0_ref, y1_ref, g_ref, o_ref):
    o_ref[...] = _norm_rows(x_ref[...] + y0_ref[...] + y1_ref[...], g_ref[...])


def final_rmsnorm(h, addends, gain):
    n, d = h.shape
    tm = _row_tile(n)
    nblk = n // tm
    return pl.pallas_call(
        _final_norm_kernel,
        grid=(nblk,),
        in_specs=[pl.BlockSpec((tm, d), lambda i: (i, 0)),
                  pl.BlockSpec((tm, d), lambda i: (i, 0)),
                  pl.BlockSpec((tm, d), lambda i: (nblk + i, 0)),
                  pl.BlockSpec((1, d), lambda i: (0, 0))],
        out_specs=pl.BlockSpec((tm, d), lambda i: (i, 0)),
        out_shape=jax.ShapeDtypeStruct((n, d), F32),
        compiler_params=_params("parallel"),
        name="final_norm",
    )(h, addends, addends, gain.reshape(1, d).astype(F32))


def _swiglu_part(xn, wg, wu, wd):
    g = jnp.dot(xn, wg, preferred_element_type=F32)
    u = jnp.dot(xn, wu, preferred_element_type=F32)
    a = (g * jax.nn.sigmoid(g) * u).astype(BF16)
    return jnp.dot(a, wd, preferred_element_type=F32)


def _dense_ffn_kernel(x_ref, g_ref, wg_ref, wu_ref, wd_ref, o_ref, xn_ref):
    j = pl.program_id(1)

    @pl.when(j == 0)
    def _():
        xn_ref[...] = _norm_rows(x_ref[...], g_ref[...]).astype(BF16)

    part = _swiglu_part(xn_ref[...], wg_ref[...], wu_ref[...], wd_ref[...])

    @pl.when(j == 0)
    def _():
        o_ref[...] = x_ref[...] + part

    @pl.when(j > 0)
    def _():
        o_ref[...] += part


def dense_ffn(h, gain, w_gate_up, w_down, layer):
    n, d = h.shape
    dff = w_down.shape[1]
    tm = _row_tile(n)
    tf = _pick(dff, (512, 256, 128))
    nj = dff // tf
    return pl.pallas_call(
        _dense_ffn_kernel,
        grid=(n // tm, nj),
        in_specs=[pl.BlockSpec((tm, d), lambda i, j: (i, 0)),
                  pl.BlockSpec((1, d), lambda i, j: (0, 0)),
                  pl.BlockSpec((None, d, tf), lambda i, j: (layer, 0, j)),
                  pl.BlockSpec((None, d, tf), lambda i, j: (layer, 0, nj + j)),
                  pl.BlockSpec((None, tf, d), lambda i, j: (layer, j, 0))],
        out_specs=pl.BlockSpec((tm, d), lambda i, j: (i, 0)),
        out_shape=jax.ShapeDtypeStruct((n, d), F32),
        scratch_shapes=[pltpu.VMEM((tm, d), BF16)],
        compiler_params=_params("parallel", "arbitrary"),
        name="dense_ffn",
    )(h, gain.reshape(1, d).astype(F32), w_gate_up, w_gate_up, w_down)


def _moe_ffn_kernel(be_ref, nused_ref, tok_ref, dst_ref,
                    h_hbm, gain_ref, gate_ref, wg_ref, wu_ref, wd_ref, y_hbm,
                    xbuf, xn_ref, acc_ref, ybuf, gsem, ssem, *, tm, nj, rows_per_step):
    b = pl.program_id(0)
    j = pl.program_id(1)
    n_used = nused_ref[0]

    def gather_rows(blk, lo, hi):
        slot = blk % 2

        def body(r, c):
            tok = tok_ref[blk * tm + r]
            pltpu.make_async_copy(h_hbm.at[pl.ds(tok, 1)], xbuf.at[slot, pl.ds(r, 1)],
                                  gsem.at[slot]).start()
            return c

        lax.fori_loop(lo, hi, body, 0)

    def scatter_rows(blk, lo, hi):
        def body(r, c):
            dst = dst_ref[blk * tm + r]
            pltpu.make_async_copy(ybuf.at[pl.ds(r, 1)], y_hbm.at[pl.ds(dst, 1)], ssem.at[0]).start()
            return c

        lax.fori_loop(lo, hi, body, 0)

    def wait_scatter():
        pltpu.make_async_copy(ybuf, y_hbm.at[pl.ds(0, tm)], ssem.at[0]).wait()

    @pl.when(b < n_used)
    def _():
        lo = j * rows_per_step
        hi = jnp.minimum(lo + rows_per_step, tm)

        @pl.when(jnp.logical_and(b == 0, j == 0))
        def _():
            gather_rows(0, 0, tm)
            ybuf[...] = jnp.zeros_like(ybuf)
            init = pltpu.make_async_copy(ybuf, y_hbm.at[pl.ds(y_hbm.shape[0] - tm, tm)], ssem.at[0])
            init.start()
            init.wait()

        @pl.when(j == 0)
        def _():
            slot = b % 2
            pltpu.make_async_copy(h_hbm.at[pl.ds(0, tm)], xbuf.at[slot], gsem.at[slot]).wait()
            xn_ref[...] = _norm_rows(xbuf[slot], gain_ref[...]).astype(BF16)

        @pl.when(b + 1 < n_used)
        def _():
            gather_rows(b + 1, lo, hi)

        @pl.when(b > 0)
        def _():
            scatter_rows(b - 1, lo, hi)

        part = _swiglu_part(xn_ref[...], wg_ref[0], wu_ref[0], wd_ref[0])

        @pl.when(j == 0)
        def _():
            acc_ref[...] = part

        @pl.when(j > 0)
        def _():
            acc_ref[...] += part

        @pl.when(j == nj - 1)
        def _():
            @pl.when(b > 0)
            def _():
                wait_scatter()

            ybuf[...] = acc_ref[...] * gate_ref[...]

            @pl.when(b == n_used - 1)
            def _():
                scatter_rows(b, 0, tm)
                wait_scatter()


def moe_ffn(h, gain, blk_e, n_used, row_tok, row_dst, row_gate, w_gate, w_up, w_down, tm):
    n, d = h.shape
    dff = w_down.shape[1]
    tf = _pick(dff, (512, 256, 128))
    nj = dff // tf
    nb = row_tok.shape[0] // tm
    rows_per_step = -(-tm // nj)

    def jj(b, j, nused):
        return jnp.where(b < nused[0], j, nj - 1)

    grid_spec = pltpu.PrefetchScalarGridSpec(
        num_scalar_prefetch=4,
        grid=(nb, nj),
        in_specs=[pl.BlockSpec(memory_space=pl.ANY),
                  pl.BlockSpec((1, d), lambda b, j, be, nu, *_: (0, 0)),
                  pl.BlockSpec((tm, 1), lambda b, j, be, nu, *_: (b, 0)),
                  pl.BlockSpec((1, d, tf), lambda b, j, be, nu, *_: (be[b], 0, jj(b, j, nu))),
                  pl.BlockSpec((1, d, tf), lambda b, j, be, nu, *_: (be[b], 0, jj(b, j, nu))),
                  pl.BlockSpec((1, tf, d), lambda b, j, be, nu, *_: (be[b], jj(b, j, nu), 0))],
        out_specs=pl.BlockSpec(memory_space=pl.ANY),
        scratch_shapes=[pltpu.VMEM((2, tm, d), F32), pltpu.VMEM((tm, d), BF16),
                        pltpu.VMEM((tm, d), F32), pltpu.VMEM((tm, d), F32),
                        pltpu.SemaphoreType.DMA((2,)), pltpu.SemaphoreType.DMA((1,))],
    )
    return pl.pallas_call(
        functools.partial(_moe_ffn_kernel, tm=tm, nj=nj, rows_per_step=rows_per_step),
        grid_spec=grid_spec,
        out_shape=jax.ShapeDtypeStruct((TOP_K * n + tm, d), F32),
        compiler_params=_params("arbitrary", "arbitrary"),
        name="moe_ffn",
    )(blk_e, n_used, row_tok, row_dst, h, gain.reshape(1, d).astype(F32),
      row_gate.reshape(-1, 1), w_gate, w_up, w_down)


def _qk(q, k):
    return lax.dot_general(q, k, (((1,), (1,)), ((), ())), preferred_element_type=F32)


def _pv(p, v):
    return jnp.dot(p.astype(BF16), v, preferred_element_type=F32)


def _flash_update(s, v, m, l, acc):
    m_new = jnp.maximum(m, jnp.max(s, axis=-1, keepdims=True))
    alpha = jnp.exp2(m - m_new)
    p = jnp.exp2(s - m_new)
    l = alpha * l + jnp.sum(p, axis=-1, keepdims=True)
    acc = alpha * acc + _pv(p, v)
    return m_new, l, acc


def _iota2(shape):
    return (lax.broadcasted_iota(jnp.int32, shape, 0), lax.broadcasted_iota(jnp.int32, shape, 1))


def _query_tile(seq):
    return _pick(seq, (512, 256, 128))


def _causal_sweep(q_ref, k_ref, v_ref, o_ref, col_bias, nq, tq):
    r16, c16 = _iota2((N_META, BLOCK))
    k0 = k_ref[0:BLOCK, :]
    v0 = v_ref[0:BLOCK, :]

    s = _qk(q_ref[0:N_META, :], k0) + col_bias(None)
    s = jnp.where(c16 <= r16, s, NEG)
    p = jnp.exp2(s - jnp.max(s, axis=-1, keepdims=True))
    o = _pv(p, v0) * (1.0 / jnp.sum(p, axis=-1, keepdims=True))
    o_ref[0:N_META, :] = o.astype(o_ref.dtype)

    def qblock(i, carry):
        qs = pl.multiple_of(N_META + i * tq, N_META)
        q = q_ref[pl.ds(qs, tq), :]
        cm = lax.broadcasted_iota(jnp.int32, (tq, BLOCK), 1)
        s = jnp.where(cm < N_META, _qk(q, k0) + col_bias(None), NEG)
        m = jnp.max(s, axis=-1, keepdims=True)
        p = jnp.exp2(s - m)
        l = jnp.sum(p, axis=-1, keepdims=True)
        acc = _pv(p, v0)

        def chunk(t, mla):
            ks = pl.multiple_of(N_META + t * tq, N_META)
            s = _qk(q, k_ref[pl.ds(ks, tq), :]) + col_bias(t)
            return _flash_update(s, v_ref[pl.ds(ks, tq), :], *mla)

        m, l, acc = lax.fori_loop(0, i, chunk, (m, l, acc))
        r, c = _iota2((tq, tq))
        s = _qk(q, k_ref[pl.ds(qs, tq), :]) + col_bias(i)
        s = jnp.where(c <= r, s, NEG)
        m, l, acc = _flash_update(s, v_ref[pl.ds(qs, tq), :], m, l, acc)
        o_ref[pl.ds(qs, tq), :] = (acc * (1.0 / l)).astype(o_ref.dtype)
        return carry

    lax.fori_loop(0, nq, qblock, 0)


def _fox_kernel(q_ref, k_ref, v_ref, cm_ref, cr_ref, o_ref, *, nq, tq):
    def col_bias(t):
        return -cm_ref[...] if t is None else -cr_ref[pl.ds(t, 1), :]

    _causal_sweep(q_ref, k_ref, v_ref, o_ref, col_bias, nq, tq)


def fox_attention(proj, c_meta, c_real, b, l):
    nq, tq = c_real.shape[2:]
    h = FOX_HEADS
    p3 = proj.reshape(b, l, 3 * h * FOX_DIM)
    blk = lambda off: pl.BlockSpec((None, l, FOX_DIM), lambda bi, hi: (bi, 0, off + hi))
    out = pl.pallas_call(
        functools.partial(_fox_kernel, nq=nq, tq=tq),
        grid=(b, h),
        in_specs=[blk(0), blk(h), blk(2 * h),
                  pl.BlockSpec((None, None, 1, BLOCK), lambda bi, hi: (bi, hi, 0, 0)),
                  pl.BlockSpec((None, None, nq, tq), lambda bi, hi: (bi, hi, 0, 0))],
        out_specs=pl.BlockSpec((None, l, FOX_DIM), lambda bi, hi: (bi, 0, hi)),
        out_shape=jax.ShapeDtypeStruct((b, l, h * FOX_DIM), BF16),
        compiler_params=_params("parallel", "parallel"),
        name="fox_attention",
    )(p3, p3, p3, c_meta, c_real)
    return out.reshape(b * l, h * FOX_DIM)


def _fox_decay_kernel(g_ref, bf_ref, c_ref, *, nchunk):
    r, c = _iota2((BLOCK, BLOCK))
    tri = jnp.where(c <= r, 1.0, 0.0).astype(F32)
    bf = bf_ref[...]

    def log_sigmoid(x):
        return jnp.minimum(x, 0.0) - jnp.log1p(jnp.exp(-jnp.abs(x)))

    def chunk_cumsum(start):
        lf = log_sigmoid(g_ref[pl.ds(start, BLOCK), :] + bf)
        return jnp.dot(tri, lf, preferred_element_type=F32, precision=lax.Precision.HIGHEST)

    head = chunk_cumsum(0)[0:N_META, :]
    c_ref[0:N_META, :] = head * LOG2E

    def body(t, carry):
        start = pl.multiple_of(N_META + t * BLOCK, N_META)
        cc = chunk_cumsum(start) + carry
        c_ref[pl.ds(start, BLOCK), :] = cc * LOG2E
        return cc[BLOCK - 1:BLOCK, :]

    lax.fori_loop(0, nchunk, body, head[N_META - 1:N_META, :])


def fox_decay(gate_logit, b_f_padded, b, l):
    g3 = gate_logit.reshape(b, l, LANES)
    return pl.pallas_call(
        functools.partial(_fox_decay_kernel, nchunk=(l - N_META) // BLOCK),
        grid=(b,),
        in_specs=[pl.BlockSpec((None, l, LANES), lambda bi: (bi, 0, 0)),
                  pl.BlockSpec((1, LANES), lambda bi: (0, 0))],
        out_specs=pl.BlockSpec((None, l, LANES), lambda bi: (bi, 0, 0)),
        out_shape=jax.ShapeDtypeStruct((b, l, LANES), F32),
        compiler_params=_params("parallel"),
        name="fox_decay",
    )(g3, b_f_padded)


def _mla_kernel(q_ref, kv_ref, kr_ref, cs_ref, o_ref, qx_ref, kx_ref, *, nq, tq):
    cs = cs_ref[...]
    lane = lax.broadcasted_iota(jnp.int32, cs.shape, 1)
    eq = q_ref[:, QK_NOPE:].astype(F32) * cs
    qx_ref[:, :QK_NOPE] = q_ref[:, :QK_NOPE]
    qx_ref[:, QK_NOPE:] = (eq + pltpu.roll(eq, QK_ROPE, axis=1)).astype(BF16)
    ek = kr_ref[...].astype(F32) * cs
    rk = jnp.where(lane < QK_ROPE, ek + pltpu.roll(ek, QK_ROPE, axis=1), 0.0)
    kx_ref[:, :QK_NOPE] = kv_ref[:, :QK_NOPE]
    kx_ref[:, QK_NOPE:] = rk.astype(BF16)
    v_ref = kv_ref.at[:, QK_NOPE:]
    _causal_sweep(qx_ref, kx_ref, v_ref, o_ref, lambda t: 0.0, nq, tq)


def mla_attention(qa, kv, proj, kr_col, cs, b, l):
    tq = _query_tile(l - N_META)
    nq = (l - N_META) // tq
    h = MLA_HEADS
    w = QK_NOPE + 2 * QK_ROPE
    qa3 = qa.reshape(b, l, h * w)
    kv3 = kv.reshape(b, l, h * (QK_NOPE + V_DIM))
    p3 = proj.reshape(b, l, proj.shape[1])
    out = pl.pallas_call(
        functools.partial(_mla_kernel, nq=nq, tq=tq),
        grid=(b, h),
        in_specs=[pl.BlockSpec((None, l, w), lambda bi, hi: (bi, 0, hi)),
                  pl.BlockSpec((None, l, QK_NOPE + V_DIM), lambda bi, hi: (bi, 0, hi)),
                  pl.BlockSpec((None, l, 2 * QK_ROPE), lambda bi, hi: (bi, 0, kr_col)),
                  pl.BlockSpec((l, 2 * QK_ROPE), lambda bi, hi: (0, 0))],
        out_specs=pl.BlockSpec((None, l, V_DIM), lambda bi, hi: (bi, 0, hi)),
        out_shape=jax.ShapeDtypeStruct((b, l, h * V_DIM), BF16),
        scratch_shapes=[pltpu.VMEM((l, w), BF16), pltpu.VMEM((l, w), BF16)],
        compiler_params=_params("parallel", "parallel"),
        name="mla_attention",
    )(qa3, kv3, p3, cs)
    return out.reshape(b * l, h * V_DIM)


def _swa_kernel(sink_ref, slope_ref, q_ref, k_ref, v_ref, o_ref, *, nb):
    kvh = pl.program_id(1)
    r16, c16 = _iota2((N_META, BLOCK))
    r, c = _iota2((BLOCK, BLOCK))
    rel = (r - c).astype(F32)
    rel16 = (r16 - c16).astype(F32)
    k0 = k_ref[0:BLOCK, :]
    v0 = v_ref[0:BLOCK, :]
    sinks = [sink_ref[kvh * SWA_GROUP + g] for g in range(SWA_GROUP)]
    slopes = [slope_ref[kvh * SWA_GROUP + g] for g in range(SWA_GROUP)]

    for g in range(SWA_GROUP):
        cols = slice(g * SWA_DIM, (g + 1) * SWA_DIM)
        s = _qk(q_ref[0:N_META, cols], k0) - slopes[g] * rel16
        s = jnp.where(c16 <= r16, s, NEG)
        m = jnp.maximum(jnp.max(s, axis=-1, keepdims=True), sinks[g])
        p = jnp.exp2(s - m)
        l = jnp.sum(p, axis=-1, keepdims=True) + jnp.exp2(sinks[g] - m)
        o_ref[0:N_META, cols] = (_pv(p, v0) * (1.0 / l)).astype(o_ref.dtype)

    def qblock(i, carry):
        qs = pl.multiple_of(N_META + i * BLOCK, N_META)
        ps = pl.multiple_of(jnp.where(i > 0, qs - BLOCK, qs), N_META)
        kp = k_ref[pl.ds(ps, BLOCK), :]
        vp = v_ref[pl.ds(ps, BLOCK), :]
        kc = k_ref[pl.ds(qs, BLOCK), :]
        vc = v_ref[pl.ds(qs, BLOCK), :]
        dist_m = rel + jnp.asarray(N_META + i * BLOCK, F32)
        dist_p = rel + float(BLOCK)
        prev_ok = jnp.logical_and(c > r, i > 0)
        for g in range(SWA_GROUP):
            cols = slice(g * SWA_DIM, (g + 1) * SWA_DIM)
            q = q_ref[pl.ds(qs, BLOCK), cols]
            s_m = jnp.where(c < N_META, _qk(q, k0) - slopes[g] * dist_m, NEG)
            s_p = jnp.where(prev_ok, _qk(q, kp) - slopes[g] * dist_p, NEG)
            s_c = jnp.where(c <= r, _qk(q, kc) - slopes[g] * rel, NEG)
            m = jnp.max(jnp.maximum(jnp.maximum(s_m, s_p), s_c), axis=-1, keepdims=True)
            m = jnp.maximum(m, sinks[g])
            p_m, p_p, p_c = jnp.exp2(s_m - m), jnp.exp2(s_p - m), jnp.exp2(s_c - m)
            l = jnp.sum(p_m + p_p + p_c, axis=-1, keepdims=True) + jnp.exp2(sinks[g] - m)
            o = _pv(p_m, v0) + _pv(p_p, vp) + _pv(p_c, vc)
            o_ref[pl.ds(qs, BLOCK), cols] = (o * (1.0 / l)).astype(o_ref.dtype)
        return carry

    lax.fori_loop(0, nb, qblock, 0)


def swa_attention(proj, q_col, k_col, v_col, sinks, slopes, b, l):
    nb = (l - N_META) // BLOCK
    gw = SWA_GROUP * SWA_DIM
    p3 = proj.reshape(b, l, proj.shape[1])
    grid_spec = pltpu.PrefetchScalarGridSpec(
        num_scalar_prefetch=2,
        grid=(b, SWA_KV_HEADS),
        in_specs=[pl.BlockSpec((None, l, gw), lambda bi, hi, s0, s1: (bi, 0, q_col + hi)),
                  pl.BlockSpec((None, l, SWA_DIM), lambda bi, hi, s0, s1: (bi, 0, k_col + hi)),
                  pl.BlockSpec((None, l, SWA_DIM), lambda bi, hi, s0, s1: (bi, 0, v_col + hi))],
        out_specs=pl.BlockSpec((None, l, gw), lambda bi, hi, s0, s1: (bi, 0, hi)),
    )
    out = pl.pallas_call(
        functools.partial(_swa_kernel, nb=nb),
        grid_spec=grid_spec,
        out_shape=jax.ShapeDtypeStruct((b, l, SWA_HEADS * SWA_DIM), BF16),
        compiler_params=_params("parallel", "parallel"),
        name="swa_attention",
    )(sinks, slopes, p3, p3, p3)
    return out.reshape(b * l, SWA_HEADS * SWA_DIM)


def _rope_swap_cols(w):
    half = QK_ROPE // 2
    return jnp.concatenate([-w[..., half:], w[..., :half]], axis=-1)


def _rope_table(l):
    half = QK_ROPE // 2
    inv = ROPE_THETA ** (-jnp.arange(half, dtype=F32) / half)
    ang = jnp.arange(l, dtype=F32)[:, None] * inv[None, :]
    cos, sin = jnp.cos(ang), jnp.sin(ang)
    return jnp.concatenate([cos, cos, sin, sin], axis=-1)


def even_layer(h, pending, b, l, layer, norm1, w_in, q_norm, kv_norm, w_uq, w_ukv, sinks, w_o, norm2,
               w_gate_up, w_down, cs, slopes):
    d = h.shape[1]
    o1, o2, o3 = Q_LORA, Q_LORA + KV_LORA, Q_LORA + KV_LORA + QK_ROPE
    o4 = o3 + SWA_HEADS * SWA_DIM
    o5 = o4 + SWA_KV_HEADS * SWA_DIM
    w_kr = w_in[:, o2:o3]
    w_in2 = jnp.concatenate([w_in[:, o3:o4] * (SWA_DIM ** -0.5 * LOG2E), w_in[:, :o1], w_in[:, o1:o2],
                             w_in[:, o4:o5], w_in[:, o5:], w_kr, _rope_swap_cols(w_kr)],
                            axis=1).astype(BF16)
    w_in2 = jnp.pad(w_in2, ((0, 0), (0, -w_in2.shape[1] % 512)))
    res = norm_proj(h, 0, d, norm1, w_in2, addends=pending, name="even_in_proj")
    proj = res[0]
    if pending is not None:
        h = res[1]
    cq_col = (SWA_HEADS * SWA_DIM) // Q_LORA
    ckv_col = (SWA_HEADS * SWA_DIM + Q_LORA) // KV_LORA
    ks_col = (SWA_HEADS * SWA_DIM + Q_LORA + KV_LORA) // SWA_DIM
    vs_col = ks_col + SWA_KV_HEADS
    kr_col = vs_col + SWA_KV_HEADS

    wq = w_uq.reshape(Q_LORA, MLA_HEADS, QK_NOPE + QK_ROPE) * ((QK_NOPE + QK_ROPE) ** -0.5 * LOG2E)
    wq2 = jnp.concatenate([wq, _rope_swap_cols(wq[..., QK_NOPE:])], axis=-1)
    wq2 = wq2.reshape(Q_LORA, MLA_HEADS * (QK_NOPE + 2 * QK_ROPE)).astype(BF16)
    qa = norm_proj(proj, cq_col, Q_LORA, q_norm, wq2, name="mla_q_up")[0]
    kv = norm_proj(proj, ckv_col, KV_LORA, kv_norm, w_ukv.astype(BF16), name="mla_kv_up")[0]

    o_a = mla_attention(qa, kv, proj, kr_col, cs, b, l)
    o_b = swa_attention(proj, 0, ks_col, vs_col, sinks.astype(F32) * LOG2E, slopes * LOG2E, b, l)
    na = MLA_HEADS * V_DIM
    h = out_proj([o_a, o_b], [w_o[:na].astype(BF16), w_o[na:].astype(BF16)], h, name="even_out_proj")
    return dense_ffn(h, norm2, w_gate_up, w_down, layer)


def moe_layer(h, layer, norm2, router, w_gate, w_up, w_down):
    n, d = h.shape
    ne = router.shape[1]
    router_p = jnp.zeros((d, LANES), F32).at[:, :ne].set(router)
    logits = router_logits(h, norm2, router_p)
    top_val, top_idx = lax.top_k(logits[:, :ne], TOP_K)
    gates = jax.nn.softmax(top_val, axis=-1)
    a = n * TOP_K
    tm = MOE_ROWS if a >= 8 * MOE_ROWS else 64
    flat_e = top_idx.reshape(-1)
    order = jnp.argsort(flat_e)
    se, sg = flat_e[order], gates.reshape(-1)[order]
    s_tok, s_slot = order // TOP_K, order % TOP_K
    counts = jnp.bincount(flat_e, length=ne)
    padded = (counts + tm - 1) // tm * tm
    start_sorted = jnp.cumsum(counts) - counts
    end_pad = jnp.cumsum(padded)
    start_pad = end_pad - padded
    dest = start_pad[se] + jnp.arange(a) - start_sorted[se]
    n_blocks = -(-a // tm) + ne
    p = n_blocks * tm
    row_tok = jnp.zeros((p,), jnp.int32).at[dest].set(s_tok.astype(jnp.int32))
    trash = TOP_K * n + jnp.arange(p, dtype=jnp.int32) % tm
    row_dst = trash.at[dest].set((s_slot * n + s_tok).astype(jnp.int32))
    row_gate = jnp.zeros((p,), F32).at[dest].set(sg)
    blk_start = jnp.arange(n_blocks) * tm
    blk_e = jnp.minimum(jnp.sum(blk_start[:, None] >= end_pad[None, :], axis=1), ne - 1)
    n_used = (end_pad[-1] // tm).astype(jnp.int32).reshape(1)
    return moe_ffn(h, norm2, (blk_e + layer * ne).astype(jnp.int32), n_used, row_tok, row_dst,
                   row_gate, w_gate, w_up, w_down, tm)


def odd_layer(h, b, l, layer, norm1, w_in, b_f, w_o, norm2, router, w_gate, w_up, w_down):
    d = h.shape[1]
    hd = FOX_HEADS * FOX_DIM
    tq = _query_tile(l - N_META)
    w_qkv = jnp.concatenate([w_in[:, :hd] * (FOX_DIM ** -0.5 * LOG2E), w_in[:, hd:3 * hd]],
                            axis=1).astype(BF16)
    w_f = jnp.zeros((d, LANES), F32).at[:, :FOX_HEADS].set(w_in[:, 3 * hd:]).astype(BF16)
    proj, gate_logit = norm_proj(h, 0, d, norm1, w_qkv, w_aux=w_f, name="fox_in_proj")
    b_f_p = jnp.zeros((1, LANES), F32).at[0, :FOX_HEADS].set(b_f.astype(F32))
    c = fox_decay(gate_logit, b_f_p, b, l)
    c = jnp.swapaxes(c[:, :, :FOX_HEADS], 1, 2)
    c_meta = jnp.zeros((b, FOX_HEADS, 1, BLOCK), F32).at[:, :, 0, :N_META].set(c[:, :, :N_META])
    c_real = c[:, :, N_META:].reshape(b, FOX_HEADS, (l - N_META) // tq, tq)
    o = fox_attention(proj, c_meta, c_real, b, l)
    h = out_proj([o], [w_o.astype(BF16)], h, name="fox_out_proj")
    return h, moe_layer(h, layer, norm2, router, w_gate, w_up, w_down)


def alibi_slopes(n):
    return jnp.asarray([2.0 ** (-8.0 * (i + 1) / n) for i in range(n)], dtype=F32)


def kernel(x, meta, ev_norm1, ev_w_in, ev_q_norm, ev_kv_norm, ev_w_uq, ev_w_ukv, ev_sinks, ev_w_o,
           ev_norm2, ev_w_gate_up, ev_w_down, od_norm1, od_w_in, od_b_f, od_w_o, od_norm2,
           od_router, od_w_gate, od_w_up, od_w_down, final_norm):
    b, seq, d = x.shape
    l = seq + N_META
    depth = ev_norm1.shape[0] + od_norm1.shape[0]
    assert depth % 2 == 0, "the trunk must end with an odd (MoE) layer"
    h = jnp.concatenate([jnp.broadcast_to(meta[None].astype(x.dtype), (b, N_META, d)), x], axis=1)
    h = h.reshape(b * l, d)
    cs = _rope_table(l)
    slopes = alibi_slopes(SWA_HEADS)
    ev_gu, ev_dn = ev_w_gate_up.astype(BF16), ev_w_down.astype(BF16)
    dff = od_w_gate.shape[-1]
    od_g = od_w_gate.astype(BF16).reshape(-1, d, dff)
    od_u = od_w_up.astype(BF16).reshape(-1, d, dff)
    od_d = od_w_down.astype(BF16).reshape(-1, dff, d)
    pending = None
    for layer in range(depth):
        i = layer // 2
        if layer % 2 == 0:
            h = even_layer(h, pending, b, l, i, ev_norm1[i], ev_w_in[i], ev_q_norm[i], ev_kv_norm[i],
                           ev_w_uq[i], ev_w_ukv[i], ev_sinks[i], ev_w_o[i], ev_norm2[i], ev_gu, ev_dn,
                           cs, slopes)
        else:
            h, pending = odd_layer(h, b, l, i, od_norm1[i], od_w_in[i], od_b_f[i], od_w_o[i],
                                   od_norm2[i], od_router[i], od_g, od_u, od_d)
    out = final_rmsnorm(h, pending, final_norm)
    return out.reshape(b, l, d)[:, N_META:]
```

```python
import functools

import jax
import jax.numpy as jnp
from jax import lax
from jax.experimental import pallas as pl
from jax.experimental.pallas import tpu as pltpu

F32 = jnp.float32
BF16 = jnp.bfloat16

N_META = 16
BLOCK = 128
EPS = 1e-6
MLA_HEADS = 8
Q_LORA = 512
KV_LORA = 256
QK_NOPE = 128
QK_ROPE = 64
V_DIM = 128
ROPE_THETA = 10000.0
SWA_HEADS = 8
SWA_KV_HEADS = 2
SWA_DIM = 128
SWA_GROUP = SWA_HEADS // SWA_KV_HEADS
FOX_HEADS = 16
FOX_DIM = 128
TOP_K = 2
MOE_ROWS = 512
LANES = 128
NEG = -1e30
LOG2E = 1.4426950408889634
V7X_VMEM_BYTES = 64 * 1024 * 1024
VMEM_LIMIT = V7X_VMEM_BYTES - 8 * 1024 * 1024


def _params(*sem):
    return pltpu.CompilerParams(dimension_semantics=sem, vmem_limit_bytes=VMEM_LIMIT)


def _pick(n, candidates):
    for c in candidates:
        if n % c == 0:
            return c
    return n


def _row_tile(n):
    return _pick(n, (688, 512, 384, 256, 128, 64, 32, 16))


def _norm_rows(x, g):
    ms = jnp.mean(x * x, axis=-1, keepdims=True)
    return (x * lax.rsqrt(ms + EPS)) * g


def _norm_proj_kernel(*refs, has_add, has_aux):
    refs = list(refs)
    x_ref = refs.pop(0)
    y0_ref, y1_ref = (refs.pop(0), refs.pop(0)) if has_add else (None, None)
    g_ref, w_ref = refs.pop(0), refs.pop(0)
    waux_ref = refs.pop(0) if has_aux else None
    o_ref = refs.pop(0)
    xsum_ref = refs.pop(0) if has_add else None
    oaux_ref = refs.pop(0) if has_aux else None
    xn_ref = refs.pop(0)

    @pl.when(pl.program_id(1) == 0)
    def _():
        x = x_ref[...].astype(F32)
        if has_add:
            x = x + y0_ref[...] + y1_ref[...]
            xsum_ref[...] = x
        xn = _norm_rows(x, g_ref[...]).astype(BF16)
        xn_ref[...] = xn
        if has_aux:
            oaux_ref[...] = jnp.dot(xn, waux_ref[...], preferred_element_type=F32)

    o_ref[...] = jnp.dot(xn_ref[...], w_ref[...], preferred_element_type=F32).astype(o_ref.dtype)


def norm_proj(x, xcol, k, gain, w, addends=None, w_aux=None, name="norm_proj"):
    n = x.shape[0]
    nout = w.shape[1]
    has_add = addends is not None
    has_aux = w_aux is not None
    tm = _pick(n, (384, 256, 128, 64, 32, 16)) if has_add else _row_tile(n)
    tn = _pick(nout, (1024, 512, 256, 128))
    in_specs = [pl.BlockSpec((tm, k), lambda i, j: (i, xcol))]
    args = [x]
    if has_add:
        nblk = n // tm
        in_specs += [pl.BlockSpec((tm, k), lambda i, j: (i, 0)),
                     pl.BlockSpec((tm, k), lambda i, j: (nblk + i, 0))]
        args += [addends, addends]
    in_specs += [pl.BlockSpec((1, k), lambda i, j: (0, 0)), pl.BlockSpec((k, tn), lambda i, j: (0, j))]
    args += [gain.reshape(1, k).astype(F32), w]
    if has_aux:
        in_specs.append(pl.BlockSpec((k, w_aux.shape[1]), lambda i, j: (0, 0)))
        args.append(w_aux)
    out_shape = [jax.ShapeDtypeStruct((n, nout), BF16)]
    out_specs = [pl.BlockSpec((tm, tn), lambda i, j: (i, j))]
    if has_add:
        out_shape.append(jax.ShapeDtypeStruct((n, k), F32))
        out_specs.append(pl.BlockSpec((tm, k), lambda i, j: (i, 0)))
    if has_aux:
        out_shape.append(jax.ShapeDtypeStruct((n, w_aux.shape[1]), F32))
        out_specs.append(pl.BlockSpec((tm, w_aux.shape[1]), lambda i, j: (i, 0)))
    return pl.pallas_call(
        functools.partial(_norm_proj_kernel, has_add=has_add, has_aux=has_aux),
        grid=(n // tm, nout // tn),
        in_specs=in_specs,
        out_specs=out_specs,
        out_shape=out_shape,
        scratch_shapes=[pltpu.VMEM((tm, k), BF16)],
        compiler_params=_params("parallel", "arbitrary"),
        name=name,
    )(*args)


def _out_proj_kernel(*refs, n_lhs):
    a_refs = refs[:n_lhs]
    w_refs = refs[n_lhs:2 * n_lhs]
    res_ref, o_ref = refs[2 * n_lhs:]
    acc = res_ref[...]
    for a_ref, w_ref in zip(a_refs, w_refs):
        acc = acc + jnp.dot(a_ref[...], w_ref[...], preferred_element_type=F32)
    o_ref[...] = acc


def out_proj(lhs, ws, res, name="out_proj"):
    n, d = res.shape
    tm = _row_tile(n)
    tn = _pick(d, (1024, 512, 256, 128))
    in_specs = []
    for a in lhs:
        in_specs.append(pl.BlockSpec((tm, a.shape[1]), lambda i, j: (i, 0)))
    for w in ws:
        in_specs.append(pl.BlockSpec((w.shape[0], tn), lambda i, j: (0, j)))
    in_specs.append(pl.BlockSpec((tm, tn), lambda i, j: (i, j)))
    return pl.pallas_call(
        functools.partial(_out_proj_kernel, n_lhs=len(lhs)),
        grid=(n // tm, d // tn),
        in_specs=in_specs,
        out_specs=pl.BlockSpec((tm, tn), lambda i, j: (i, j)),
        out_shape=jax.ShapeDtypeStruct((n, d), F32),
        compiler_params=_params("parallel", "parallel"),
        name=name,
    )(*lhs, *ws, res)


def _router_kernel(x_ref, g_ref, wr_ref, logit_ref):
    xn = _norm_rows(x_ref[...], g_ref[...])
    logit_ref[...] = jnp.dot(xn, wr_ref[...], preferred_element_type=F32,
                             precision=lax.Precision.HIGHEST)


def router_logits(h, gain, router_padded):
    n, d = h.shape
    tm = _row_tile(n)
    ne = router_padded.shape[1]
    return pl.pallas_call(
        _router_kernel,
        grid=(n // tm,),
        in_specs=[pl.BlockSpec((tm, d), lambda i: (i, 0)),
                  pl.BlockSpec((1, d), lambda i: (0, 0)),
                  pl.BlockSpec((d, ne), lambda i: (0, 0))],
        out_specs=pl.BlockSpec((tm, ne), lambda i: (i, 0)),
        out_shape=jax.ShapeDtypeStruct((n, ne), F32),
        compiler_params=_params("parallel"),
        name="router_logits",
    )(h, gain.reshape(1, d).astype(F32), router_padded)


def _final_norm_kernel(x_ref, y0_ref, y1_ref, g_ref, o_ref):
    o_ref[...] = _norm_rows(x_ref[...] + y0_ref[...] + y1_ref[...], g_ref[...])


def final_rmsnorm(h, addends, gain):
    n, d = h.shape
    tm = _row_tile(n)
    nblk = n // tm
    return pl.pallas_call(
        _final_norm_kernel,
        grid=(nblk,),
        in_specs=[pl.BlockSpec((tm, d), lambda i: (i, 0)),
                  pl.BlockSpec((tm, d), lambda i: (i, 0)),
                  pl.BlockSpec((tm, d), lambda i: (nblk + i, 0)),
                  pl.BlockSpec((1, d), lambda i: (0, 0))],
        out_specs=pl.BlockSpec((tm, d), lambda i: (i, 0)),
        out_shape=jax.ShapeDtypeStruct((n, d), F32),
        compiler_params=_params("parallel"),
        name="final_norm",
    )(h, addends, addends, gain.reshape(1, d).astype(F32))


def _swiglu_part(xn, wg, wu, wd):
    g = jnp.dot(xn, wg, preferred_element_type=F32)
    u = jnp.dot(xn, wu, preferred_element_type=F32)
    a = (g * jax.nn.sigmoid(g) * u).astype(BF16)
    return jnp.dot(a, wd, preferred_element_type=F32)


def _dense_ffn_kernel(x_ref, g_ref, wg_ref, wu_ref, wd_ref, o_ref, xn_ref):
    j = pl.program_id(1)

    @pl.when(j == 0)
    def _():
        xn_ref[...] = _norm_rows(x_ref[...], g_ref[...]).astype(BF16)

    part = _swiglu_part(xn_ref[...], wg_ref[...], wu_ref[...], wd_ref[...])

    @pl.when(j == 0)
    def _():
        o_ref[...] = x_ref[...] + part

    @pl.when(j > 0)
    def _():
        o_ref[...] += part


def dense_ffn(h, gain, w_gate_up, w_down, layer):
    n, d = h.shape
    dff = w_down.shape[1]
    tm = _row_tile(n)
    tf = _pick(dff, (512, 256, 128))
    nj = dff // tf
    return pl.pallas_call(
        _dense_ffn_kernel,
        grid=(n // tm, nj),
        in_specs=[pl.BlockSpec((tm, d), lambda i, j: (i, 0)),
                  pl.BlockSpec((1, d), lambda i, j: (0, 0)),
                  pl.BlockSpec((None, d, tf), lambda i, j: (layer, 0, j)),
                  pl.BlockSpec((None, d, tf), lambda i, j: (layer, 0, nj + j)),
                  pl.BlockSpec((None, tf, d), lambda i, j: (layer, j, 0))],
        out_specs=pl.BlockSpec((tm, d), lambda i, j: (i, 0)),
        out_shape=jax.ShapeDtypeStruct((n, d), F32),
        scratch_shapes=[pltpu.VMEM((tm, d), BF16)],
        compiler_params=_params("parallel", "arbitrary"),
        name="dense_ffn",
    )(h, gain.reshape(1, d).astype(F32), w_gate_up, w_gate_up, w_down)


def _moe_ffn_kernel(be_ref, nused_ref, tok_ref, dst_ref,
                    h_hbm, gain_ref, gate_ref, wg_ref, wu_ref, wd_ref, y_hbm,
                    xbuf, xn_ref, acc_ref, ybuf, gsem, ssem, *, tm, nj, rows_per_step):
    b = pl.program_id(0)
    j = pl.program_id(1)
    n_used = nused_ref[0]

    def gather_rows(blk, lo, hi):
        slot = blk % 2

        def body(r, c):
            tok = tok_ref[blk * tm + r]
            pltpu.make_async_copy(h_hbm.at[pl.ds(tok, 1)], xbuf.at[slot, pl.ds(r, 1)],
                                  gsem.at[slot]).start()
            return c

        lax.fori_loop(lo, hi, body, 0)

    def scatter_rows(blk, lo, hi):
        def body(r, c):
            dst = dst_ref[blk * tm + r]
            pltpu.make_async_copy(ybuf.at[pl.ds(r, 1)], y_hbm.at[pl.ds(dst, 1)], ssem.at[0]).start()
            return c

        lax.fori_loop(lo, hi, body, 0)

    def wait_scatter():
        pltpu.make_async_copy(ybuf, y_hbm.at[pl.ds(0, tm)], ssem.at[0]).wait()

    def wait_gather(slot):
        pltpu.make_async_copy(h_hbm.at[pl.ds(0, tm)], xbuf.at[slot], gsem.at[slot]).wait()

    trash0 = y_hbm.shape[0] - tm
    nxt = jnp.minimum(b + 1, n_used - 1)
    prv = jnp.maximum(b - 1, 0)

    def move_rows(rows):
        for r in rows:
            pltpu.make_async_copy(h_hbm.at[pl.ds(tok_ref[nxt * tm + r], 1)],
                                  xbuf.at[(b + 1) % 2, pl.ds(r, 1)], gsem.at[(b + 1) % 2]).start()
            dst = jnp.where(b > 0, dst_ref[prv * tm + r], trash0 + r)
            pltpu.make_async_copy(ybuf.at[pl.ds(r, 1)], y_hbm.at[pl.ds(dst, 1)], ssem.at[0]).start()

    @pl.when(b < n_used)
    def _():
        @pl.when(jnp.logical_and(b == 0, j == 0))
        def _():
            gather_rows(0, 0, tm)
            ybuf[...] = jnp.zeros_like(ybuf)
            init = pltpu.make_async_copy(ybuf, y_hbm.at[pl.ds(trash0, tm)], ssem.at[0])
            init.start()
            init.wait()

        @pl.when(j == 0)
        def _():
            wait_gather(b % 2)
            xn_ref[...] = _norm_rows(xbuf[b % 2], gain_ref[...]).astype(BF16)
            move_rows([nj * rows_per_step + r for r in range(tm - nj * rows_per_step)])

        move_rows([j * rows_per_step + k for k in range(rows_per_step)])
        part = _swiglu_part(xn_ref[...], wg_ref[0], wu_ref[0], wd_ref[0])

        @pl.when(j == 0)
        def _():
            acc_ref[...] = part

        @pl.when(j > 0)
        def _():
            acc_ref[...] += part

        @pl.when(j == nj - 1)
        def _():
            wait_scatter()
            ybuf[...] = acc_ref[...] * gate_ref[...]

            @pl.when(b == n_used - 1)
            def _():
                wait_gather((b + 1) % 2)
                scatter_rows(b, 0, tm)
                wait_scatter()


def moe_ffn(h, gain, blk_e, n_used, row_tok, row_dst, row_gate, w_gate, w_up, w_down, tm):
    n, d = h.shape
    dff = w_down.shape[1]
    tf = _pick(dff, (1024, 512, 256, 128))
    nj = dff // tf
    nb = row_tok.shape[0] // tm
    rows_per_step = tm // nj

    def jj(b, j, nused):
        return jnp.where(b < nused[0], j, nj - 1)

    grid_spec = pltpu.PrefetchScalarGridSpec(
        num_scalar_prefetch=4,
        grid=(nb, nj),
        in_specs=[pl.BlockSpec(memory_space=pl.ANY),
                  pl.BlockSpec((1, d), lambda b, j, be, nu, *_: (0, 0)),
                  pl.BlockSpec((tm, 1), lambda b, j, be, nu, *_: (b, 0)),
                  pl.BlockSpec((1, d, tf), lambda b, j, be, nu, *_: (be[b], 0, jj(b, j, nu))),
                  pl.BlockSpec((1, d, tf), lambda b, j, be, nu, *_: (be[b], 0, jj(b, j, nu))),
                  pl.BlockSpec((1, tf, d), lambda b, j, be, nu, *_: (be[b], jj(b, j, nu), 0))],
        out_specs=pl.BlockSpec(memory_space=pl.ANY),
        scratch_shapes=[pltpu.VMEM((2, tm, d), F32), pltpu.VMEM((tm, d), BF16),
                        pltpu.VMEM((tm, d), F32), pltpu.VMEM((tm, d), F32),
                        pltpu.SemaphoreType.DMA((2,)), pltpu.SemaphoreType.DMA((1,))],
    )
    return pl.pallas_call(
        functools.partial(_moe_ffn_kernel, tm=tm, nj=nj, rows_per_step=rows_per_step),
        grid_spec=grid_spec,
        out_shape=jax.ShapeDtypeStruct((TOP_K * n + tm, d), F32),
        compiler_params=_params("arbitrary", "arbitrary"),
        name="moe_ffn",
    )(blk_e, n_used, row_tok, row_dst, h, gain.reshape(1, d).astype(F32),
      row_gate.reshape(-1, 1), w_gate, w_up, w_down)


def _qk(q, k):
    return lax.dot_general(q, k, (((1,), (1,)), ((), ())), preferred_element_type=F32)


def _pv(p, v):
    return jnp.dot(p.astype(BF16), v, preferred_element_type=F32)


def _flash_update(s, v, m, l, acc):
    m_new = jnp.maximum(m, jnp.max(s, axis=-1, keepdims=True))
    alpha = jnp.exp2(m - m_new)
    p = jnp.exp2(s - m_new)
    l = alpha * l + jnp.sum(p, axis=-1, keepdims=True)
    acc = alpha * acc + _pv(p, v)
    return m_new, l, acc


def _iota2(shape):
    return (lax.broadcasted_iota(jnp.int32, shape, 0), lax.broadcasted_iota(jnp.int32, shape, 1))


def _query_tile(seq):
    return _pick(seq, (512, 256, 128))


def _causal_sweep(q_ref, k_ref, v_ref, o_ref, col_bias, nq, tq):
    r16, c16 = _iota2((N_META, BLOCK))
    k0 = k_ref[0:BLOCK, :]
    v0 = v_ref[0:BLOCK, :]

    s = _qk(q_ref[0:N_META, :], k0) + col_bias(None)
    s = jnp.where(c16 <= r16, s, NEG)
    p = jnp.exp2(s - jnp.max(s, axis=-1, keepdims=True))
    o = _pv(p, v0) * (1.0 / jnp.sum(p, axis=-1, keepdims=True))
    o_ref[0:N_META, :] = o.astype(o_ref.dtype)

    def qblock(i, carry):
        qs = pl.multiple_of(N_META + i * tq, N_META)
        q = q_ref[pl.ds(qs, tq), :]
        cm = lax.broadcasted_iota(jnp.int32, (tq, BLOCK), 1)
        s = jnp.where(cm < N_META, _qk(q, k0) + col_bias(None), NEG)
        m = jnp.max(s, axis=-1, keepdims=True)
        p = jnp.exp2(s - m)
        l = jnp.sum(p, axis=-1, keepdims=True)
        acc = _pv(p, v0)

        def chunk(t, mla):
            ks = pl.multiple_of(N_META + t * tq, N_META)
            s = _qk(q, k_ref[pl.ds(ks, tq), :]) + col_bias(t)
            return _flash_update(s, v_ref[pl.ds(ks, tq), :], *mla)

        m, l, acc = lax.fori_loop(0, i, chunk, (m, l, acc))
        r, c = _iota2((tq, tq))
        s = _qk(q, k_ref[pl.ds(qs, tq), :]) + col_bias(i)
        s = jnp.where(c <= r, s, NEG)
        m, l, acc = _flash_update(s, v_ref[pl.ds(qs, tq), :], m, l, acc)
        o_ref[pl.ds(qs, tq), :] = (acc * (1.0 / l)).astype(o_ref.dtype)
        return carry

    lax.fori_loop(0, nq, qblock, 0)


def _fox_kernel(q_ref, k_ref, v_ref, cm_ref, cr_ref, o_ref, *, nq, tq):
    def col_bias(t):
        return -cm_ref[...] if t is None else -cr_ref[pl.ds(t, 1), :]

    _causal_sweep(q_ref, k_ref, v_ref, o_ref, col_bias, nq, tq)


def fox_attention(proj, c_meta, c_real, b, l):
    nq, tq = c_real.shape[2:]
    h = FOX_HEADS
    p3 = proj.reshape(b, l, 3 * h * FOX_DIM)
    blk = lambda off: pl.BlockSpec((None, l, FOX_DIM), lambda bi, hi: (bi, 0, off + hi))
    out = pl.pallas_call(
        functools.partial(_fox_kernel, nq=nq, tq=tq),
        grid=(b, h),
        in_specs=[blk(0), blk(h), blk(2 * h),
                  pl.BlockSpec((None, None, 1, BLOCK), lambda bi, hi: (bi, hi, 0, 0)),
                  pl.BlockSpec((None, None, nq, tq), lambda bi, hi: (bi, hi, 0, 0))],
        out_specs=pl.BlockSpec((None, l, FOX_DIM), lambda bi, hi: (bi, 0, hi)),
        out_shape=jax.ShapeDtypeStruct((b, l, h * FOX_DIM), BF16),
        compiler_params=_params("parallel", "parallel"),
        name="fox_attention",
    )(p3, p3, p3, c_meta, c_real)
    return out.reshape(b * l, h * FOX_DIM)


def _fox_decay_kernel(g_ref, bf_ref, c_ref, *, nchunk):
    r, c = _iota2((BLOCK, BLOCK))
    tri = jnp.where(c <= r, 1.0, 0.0).astype(F32)
    bf = bf_ref[...]

    def log_sigmoid(x):
        return jnp.minimum(x, 0.0) - jnp.log1p(jnp.exp(-jnp.abs(x)))

    def chunk_cumsum(start):
        lf = log_sigmoid(g_ref[pl.ds(start, BLOCK), :] + bf)
        return jnp.dot(tri, lf, preferred_element_type=F32, precision=lax.Precision.HIGHEST)

    head = chunk_cumsum(0)[0:N_META, :]
    c_ref[0:N_META, :] = head * LOG2E

    def body(t, carry):
        start = pl.multiple_of(N_META + t * BLOCK, N_META)
        cc = chunk_cumsum(start) + carry
        c_ref[pl.ds(start, BLOCK), :] = cc * LOG2E
        return cc[BLOCK - 1:BLOCK, :]

    lax.fori_loop(0, nchunk, body, head[N_META - 1:N_META, :])


def fox_decay(gate_logit, b_f_padded, b, l):
    g3 = gate_logit.reshape(b, l, LANES)
    return pl.pallas_call(
        functools.partial(_fox_decay_kernel, nchunk=(l - N_META) // BLOCK),
        grid=(b,),
        in_specs=[pl.BlockSpec((None, l, LANES), lambda bi: (bi, 0, 0)),
                  pl.BlockSpec((1, LANES), lambda bi: (0, 0))],
        out_specs=pl.BlockSpec((None, l, LANES), lambda bi: (bi, 0, 0)),
        out_shape=jax.ShapeDtypeStruct((b, l, LANES), F32),
        compiler_params=_params("parallel"),
        name="fox_decay",
    )(g3, b_f_padded)


def _mla_kernel(q_ref, kv_ref, kr_ref, cs_ref, o_ref, qx_ref, kx_ref, *, nq, tq):
    cs = cs_ref[...]
    lane = lax.broadcasted_iota(jnp.int32, cs.shape, 1)
    eq = q_ref[:, QK_NOPE:].astype(F32) * cs
    qx_ref[:, :QK_NOPE] = q_ref[:, :QK_NOPE]
    qx_ref[:, QK_NOPE:] = (eq + pltpu.roll(eq, QK_ROPE, axis=1)).astype(BF16)
    ek = kr_ref[...].astype(F32) * cs
    rk = jnp.where(lane < QK_ROPE, ek + pltpu.roll(ek, QK_ROPE, axis=1), 0.0)
    kx_ref[:, :QK_NOPE] = kv_ref[:, :QK_NOPE]
    kx_ref[:, QK_NOPE:] = rk.astype(BF16)
    v_ref = kv_ref.at[:, QK_NOPE:]
    _causal_sweep(qx_ref, kx_ref, v_ref, o_ref, lambda t: 0.0, nq, tq)


def mla_attention(qa, kv, proj, kr_col, cs, b, l):
    tq = _query_tile(l - N_META)
    nq = (l - N_META) // tq
    h = MLA_HEADS
    w = QK_NOPE + 2 * QK_ROPE
    qa3 = qa.reshape(b, l, h * w)
    kv3 = kv.reshape(b, l, h * (QK_NOPE + V_DIM))
    p3 = proj.reshape(b, l, proj.shape[1])
    out = pl.pallas_call(
        functools.partial(_mla_kernel, nq=nq, tq=tq),
        grid=(b, h),
        in_specs=[pl.BlockSpec((None, l, w), lambda bi, hi: (bi, 0, hi)),
                  pl.BlockSpec((None, l, QK_NOPE + V_DIM), lambda bi, hi: (bi, 0, hi)),
                  pl.BlockSpec((None, l, 2 * QK_ROPE), lambda bi, hi: (bi, 0, kr_col)),
                  pl.BlockSpec((l, 2 * QK_ROPE), lambda bi, hi: (0, 0))],
        out_specs=pl.BlockSpec((None, l, V_DIM), lambda bi, hi: (bi, 0, hi)),
        out_shape=jax.ShapeDtypeStruct((b, l, h * V_DIM), BF16),
        scratch_shapes=[pltpu.VMEM((l, w), BF16), pltpu.VMEM((l, w), BF16)],
        compiler_params=_params("parallel", "parallel"),
        name="mla_attention",
    )(qa3, kv3, p3, cs)
    return out.reshape(b * l, h * V_DIM)


def _swa_kernel(sink_ref, slope_ref, q_ref, k_ref, v_ref, o_ref, *, nb):
    kvh = pl.program_id(1)
    r16, c16 = _iota2((N_META, BLOCK))
    r, c = _iota2((BLOCK, BLOCK))
    rel = (r - c).astype(F32)
    rel16 = (r16 - c16).astype(F32)
    k0 = k_ref[0:BLOCK, :]
    v0 = v_ref[0:BLOCK, :]
    sinks = [sink_ref[kvh * SWA_GROUP + g] for g in range(SWA_GROUP)]
    slopes = [slope_ref[kvh * SWA_GROUP + g] for g in range(SWA_GROUP)]

    for g in range(SWA_GROUP):
        cols = slice(g * SWA_DIM, (g + 1) * SWA_DIM)
        s = _qk(q_ref[0:N_META, cols], k0) - slopes[g] * rel16
        s = jnp.where(c16 <= r16, s, NEG)
        m = jnp.maximum(jnp.max(s, axis=-1, keepdims=True), sinks[g])
        p = jnp.exp2(s - m)
        l = jnp.sum(p, axis=-1, keepdims=True) + jnp.exp2(sinks[g] - m)
        o_ref[0:N_META, cols] = (_pv(p, v0) * (1.0 / l)).astype(o_ref.dtype)

    def qblock(i, carry):
        qs = pl.multiple_of(N_META + i * BLOCK, N_META)
        ps = pl.multiple_of(jnp.where(i > 0, qs - BLOCK, qs), N_META)
        kp = k_ref[pl.ds(ps, BLOCK), :]
        vp = v_ref[pl.ds(ps, BLOCK), :]
        kc = k_ref[pl.ds(qs, BLOCK), :]
        vc = v_ref[pl.ds(qs, BLOCK), :]
        dist_m = rel + jnp.asarray(N_META + i * BLOCK, F32)
        dist_p = rel + float(BLOCK)
        prev_ok = jnp.logical_and(c > r, i > 0)
        for g in range(SWA_GROUP):
            cols = slice(g * SWA_DIM, (g + 1) * SWA_DIM)
            q = q_ref[pl.ds(qs, BLOCK), cols]
            s_m = jnp.where(c < N_META, _qk(q, k0) - slopes[g] * dist_m, NEG)
            s_p = jnp.where(prev_ok, _qk(q, kp) - slopes[g] * dist_p, NEG)
            s_c = jnp.where(c <= r, _qk(q, kc) - slopes[g] * rel, NEG)
            m = jnp.max(jnp.maximum(jnp.maximum(s_m, s_p), s_c), axis=-1, keepdims=True)
            m = jnp.maximum(m, sinks[g])
            p_m, p_p, p_c = jnp.exp2(s_m - m), jnp.exp2(s_p - m), jnp.exp2(s_c - m)
            l = jnp.sum(p_m + p_p + p_c, axis=-1, keepdims=True) + jnp.exp2(sinks[g] - m)
            o = _pv(p_m, v0) + _pv(p_p, vp) + _pv(p_c, vc)
            o_ref[pl.ds(qs, BLOCK), cols] = (o * (1.0 / l)).astype(o_ref.dtype)
        return carry

    lax.fori_loop(0, nb, qblock, 0)


def swa_attention(proj, q_col, k_col, v_col, sinks, slopes, b, l):
    nb = (l - N_META) // BLOCK
    gw = SWA_GROUP * SWA_DIM
    p3 = proj.reshape(b, l, proj.shape[1])
    grid_spec = pltpu.PrefetchScalarGridSpec(
        num_scalar_prefetch=2,
        grid=(b, SWA_KV_HEADS),
        in_specs=[pl.BlockSpec((None, l, gw), lambda bi, hi, s0, s1: (bi, 0, q_col + hi)),
                  pl.BlockSpec((None, l, SWA_DIM), lambda bi, hi, s0, s1: (bi, 0, k_col + hi)),
                  pl.BlockSpec((None, l, SWA_DIM), lambda bi, hi, s0, s1: (bi, 0, v_col + hi))],
        out_specs=pl.BlockSpec((None, l, gw), lambda bi, hi, s0, s1: (bi, 0, hi)),
    )
    out = pl.pallas_call(
        functools.partial(_swa_kernel, nb=nb),
        grid_spec=grid_spec,
        out_shape=jax.ShapeDtypeStruct((b, l, SWA_HEADS * SWA_DIM), BF16),
        compiler_params=_params("parallel", "parallel"),
        name="swa_attention",
    )(sinks, slopes, p3, p3, p3)
    return out.reshape(b * l, SWA_HEADS * SWA_DIM)


def _rope_swap_cols(w):
    half = QK_ROPE // 2
    return jnp.concatenate([-w[..., half:], w[..., :half]], axis=-1)


def _rope_table(l):
    half = QK_ROPE // 2
    inv = ROPE_THETA ** (-jnp.arange(half, dtype=F32) / half)
    ang = jnp.arange(l, dtype=F32)[:, None] * inv[None, :]
    cos, sin = jnp.cos(ang), jnp.sin(ang)
    return jnp.concatenate([cos, cos, sin, sin], axis=-1)


def even_layer(h, pending, b, l, layer, norm1, w_in, q_norm, kv_norm, w_uq, w_ukv, sinks, w_o, norm2,
               w_gate_up, w_down, cs, slopes):
    d = h.shape[1]
    o1, o2, o3 = Q_LORA, Q_LORA + KV_LORA, Q_LORA + KV_LORA + QK_ROPE
    o4 = o3 + SWA_HEADS * SWA_DIM
    o5 = o4 + SWA_KV_HEADS * SWA_DIM
    w_kr = w_in[:, o2:o3]
    w_in2 = jnp.concatenate([w_in[:, o3:o4] * (SWA_DIM ** -0.5 * LOG2E), w_in[:, :o1], w_in[:, o1:o2],
                             w_in[:, o4:o5], w_in[:, o5:], w_kr, _rope_swap_cols(w_kr)],
                            axis=1).astype(BF16)
    w_in2 = jnp.pad(w_in2, ((0, 0), (0, -w_in2.shape[1] % 512)))
    res = norm_proj(h, 0, d, norm1, w_in2, addends=pending, name="even_in_proj")
    proj = res[0]
    if pending is not None:
        h = res[1]
    cq_col = (SWA_HEADS * SWA_DIM) // Q_LORA
    ckv_col = (SWA_HEADS * SWA_DIM + Q_LORA) // KV_LORA
    ks_col = (SWA_HEADS * SWA_DIM + Q_LORA + KV_LORA) // SWA_DIM
    vs_col = ks_col + SWA_KV_HEADS
    kr_col = vs_col + SWA_KV_HEADS

    wq = w_uq.reshape(Q_LORA, MLA_HEADS, QK_NOPE + QK_ROPE) * ((QK_NOPE + QK_ROPE) ** -0.5 * LOG2E)
    wq2 = jnp.concatenate([wq, _rope_swap_cols(wq[..., QK_NOPE:])], axis=-1)
    wq2 = wq2.reshape(Q_LORA, MLA_HEADS * (QK_NOPE + 2 * QK_ROPE)).astype(BF16)
    qa = norm_proj(proj, cq_col, Q_LORA, q_norm, wq2, name="mla_q_up")[0]
    kv = norm_proj(proj, ckv_col, KV_LORA, kv_norm, w_ukv.astype(BF16), name="mla_kv_up")[0]

    o_a = mla_attention(qa, kv, proj, kr_col, cs, b, l)
    o_b = swa_attention(proj, 0, ks_col, vs_col, sinks.astype(F32) * LOG2E, slopes * LOG2E, b, l)
    na = MLA_HEADS * V_DIM
    h = out_proj([o_a, o_b], [w_o[:na].astype(BF16), w_o[na:].astype(BF16)], h, name="even_out_proj")
    return dense_ffn(h, norm2, w_gate_up, w_down, layer)


def moe_layer(h, layer, norm2, router, w_gate, w_up, w_down):
    n, d = h.shape
    ne = router.shape[1]
    router_p = jnp.zeros((d, LANES), F32).at[:, :ne].set(router)
    logits = router_logits(h, norm2, router_p)
    top_val, top_idx = lax.top_k(logits[:, :ne], TOP_K)
    gates = jax.nn.softmax(top_val, axis=-1)
    a = n * TOP_K
    tm = MOE_ROWS if a >= 8 * MOE_ROWS else 64
    flat_e = top_idx.reshape(-1).astype(jnp.int32)
    _, s_pair, s_gate = lax.sort((flat_e, jnp.arange(a, dtype=jnp.int32), gates.reshape(-1)),
                                 num_keys=1, is_stable=True)
    counts = jnp.sum(flat_e[:, None] == jnp.arange(ne, dtype=jnp.int32)[None, :], axis=0)
    padded = (counts + tm - 1) // tm * tm
    start_sorted = jnp.cumsum(counts) - counts
    end_pad = jnp.cumsum(padded)
    start_pad = end_pad - padded
    n_blocks = -(-a // tm) + ne
    p = n_blocks * tm
    blk_start = jnp.arange(n_blocks) * tm
    blk_e = jnp.minimum(jnp.sum(blk_start[:, None] >= end_pad[None, :], axis=1), ne - 1)
    n_used = (end_pad[-1] // tm).astype(jnp.int32).reshape(1)
    in_blk = jnp.arange(tm, dtype=jnp.int32)[None, :]
    rank = (blk_start - start_pad[blk_e]).astype(jnp.int32)[:, None] + in_blk
    valid = (rank < counts[blk_e][:, None]).reshape(p)
    src = jnp.where(valid, (start_sorted[blk_e].astype(jnp.int32)[:, None] + rank).reshape(p), 0)
    pair = s_pair[src]
    trash = jnp.broadcast_to(TOP_K * n + in_blk, (n_blocks, tm)).reshape(p)
    row_tok = jnp.where(valid, pair // TOP_K, 0).astype(jnp.int32)
    row_dst = jnp.where(valid, (pair % TOP_K) * n + pair // TOP_K, trash).astype(jnp.int32)
    row_gate = jnp.where(valid, s_gate[src], 0.0)
    return moe_ffn(h, norm2, (blk_e + layer * ne).astype(jnp.int32), n_used, row_tok, row_dst,
                   row_gate, w_gate, w_up, w_down, tm)


def odd_layer(h, b, l, layer, norm1, w_in, b_f, w_o, norm2, router, w_gate, w_up, w_down):
    d = h.shape[1]
    hd = FOX_HEADS * FOX_DIM
    tq = _query_tile(l - N_META)
    w_qkv = jnp.concatenate([w_in[:, :hd] * (FOX_DIM ** -0.5 * LOG2E), w_in[:, hd:3 * hd]],
                            axis=1).astype(BF16)
    w_f = jnp.zeros((d, LANES), F32).at[:, :FOX_HEADS].set(w_in[:, 3 * hd:]).astype(BF16)
    proj, gate_logit = norm_proj(h, 0, d, norm1, w_qkv, w_aux=w_f, name="fox_in_proj")
    b_f_p = jnp.zeros((1, LANES), F32).at[0, :FOX_HEADS].set(b_f.astype(F32))
    c = fox_decay(gate_logit, b_f_p, b, l)
    c = jnp.swapaxes(c[:, :, :FOX_HEADS], 1, 2)
    c_meta = jnp.zeros((b, FOX_HEADS, 1, BLOCK), F32).at[:, :, 0, :N_META].set(c[:, :, :N_META])
    c_real = c[:, :, N_META:].reshape(b, FOX_HEADS, (l - N_META) // tq, tq)
    o = fox_attention(proj, c_meta, c_real, b, l)
    h = out_proj([o], [w_o.astype(BF16)], h, name="fox_out_proj")
    return h, moe_layer(h, layer, norm2, router, w_gate, w_up, w_down)


def alibi_slopes(n):
    return jnp.asarray([2.0 ** (-8.0 * (i + 1) / n) for i in range(n)], dtype=F32)


def kernel(x, meta, ev_norm1, ev_w_in, ev_q_norm, ev_kv_norm, ev_w_uq, ev_w_ukv, ev_sinks, ev_w_o,
           ev_norm2, ev_w_gate_up, ev_w_down, od_norm1, od_w_in, od_b_f, od_w_o, od_norm2,
           od_router, od_w_gate, od_w_up, od_w_down, final_norm):
    b, seq, d = x.shape
    l = seq + N_META
    depth = ev_norm1.shape[0] + od_norm1.shape[0]
    assert depth % 2 == 0, "the trunk must end with an odd (MoE) layer"
    h = jnp.concatenate([jnp.broadcast_to(meta[None].astype(x.dtype), (b, N_META, d)), x], axis=1)
    h = h.reshape(b * l, d)
    cs = _rope_table(l)
    slopes = alibi_slopes(SWA_HEADS)
    ev_gu, ev_dn = ev_w_gate_up.astype(BF16), ev_w_down.astype(BF16)
    dff = od_w_gate.shape[-1]
    od_g = od_w_gate.astype(BF16).reshape(-1, d, dff)
    od_u = od_w_up.astype(BF16).reshape(-1, d, dff)
    od_d = od_w_down.astype(BF16).reshape(-1, dff, d)
    pending = None
    for layer in range(depth):
        i = layer // 2
        if layer % 2 == 0:
            h = even_layer(h, pending, b, l, i, ev_norm1[i], ev_w_in[i], ev_q_norm[i], ev_kv_norm[i],
                           ev_w_uq[i], ev_w_ukv[i], ev_sinks[i], ev_w_o[i], ev_norm2[i], ev_gu, ev_dn,
                           cs, slopes)
        else:
            h, pending = odd_layer(h, b, l, i, od_norm1[i], od_w_in[i], od_b_f[i], od_w_o[i],
                                   od_norm2[i], od_router[i], od_g, od_u, od_d)
    out = final_rmsnorm(h, pending, final_norm)
    return out.reshape(b, l, d)[:, N_META:]
```

```python
import functools

import jax
import jax.numpy as jnp
from jax import lax
from jax.experimental import pallas as pl
from jax.experimental.pallas import tpu as pltpu

F32 = jnp.float32
BF16 = jnp.bfloat16

N_META = 16
BLOCK = 128
EPS = 1e-6
MLA_HEADS = 8
Q_LORA = 512
KV_LORA = 256
QK_NOPE = 128
QK_ROPE = 64
V_DIM = 128
ROPE_THETA = 10000.0
SWA_HEADS = 8
SWA_KV_HEADS = 2
SWA_DIM = 128
SWA_GROUP = SWA_HEADS // SWA_KV_HEADS
FOX_HEADS = 16
FOX_DIM = 128
TOP_K = 2
MOE_ROWS = 512
HEADS_PER_STEP = 2
LANES = 128
NEG = -1e30
LOG2E = 1.4426950408889634
V7X_VMEM_BYTES = 64 * 1024 * 1024
VMEM_LIMIT = V7X_VMEM_BYTES - 8 * 1024 * 1024


def _params(*sem):
    return pltpu.CompilerParams(dimension_semantics=sem, vmem_limit_bytes=VMEM_LIMIT)


def _pick(n, candidates):
    for c in candidates:
        if n % c == 0:
            return c
    return n


def _row_tile(n):
    return _pick(n, (688, 512, 384, 256, 128, 64, 32, 16))


def _norm_rows(x, g):
    ms = jnp.mean(x * x, axis=-1, keepdims=True)
    return (x * lax.rsqrt(ms + EPS)) * g


def _norm_proj_kernel(*refs, has_add, has_aux):
    refs = list(refs)
    x_ref = refs.pop(0)
    y0_ref, y1_ref = (refs.pop(0), refs.pop(0)) if has_add else (None, None)
    g_ref, w_ref = refs.pop(0), refs.pop(0)
    waux_ref = refs.pop(0) if has_aux else None
    o_ref = refs.pop(0)
    xsum_ref = refs.pop(0) if has_add else None
    oaux_ref = refs.pop(0) if has_aux else None
    xn_ref = refs.pop(0)

    @pl.when(pl.program_id(1) == 0)
    def _():
        x = x_ref[...].astype(F32)
        if has_add:
            x = x + y0_ref[...] + y1_ref[...]
            xsum_ref[...] = x
        xn = _norm_rows(x, g_ref[...]).astype(BF16)
        xn_ref[...] = xn
        if has_aux:
            oaux_ref[...] = jnp.dot(xn, waux_ref[...], preferred_element_type=F32)

    o_ref[...] = jnp.dot(xn_ref[...], w_ref[...], preferred_element_type=F32).astype(o_ref.dtype)


def norm_proj(x, xcol, k, gain, w, addends=None, w_aux=None, name="norm_proj"):
    n = x.shape[0]
    nout = w.shape[1]
    has_add = addends is not None
    has_aux = w_aux is not None
    tm = _pick(n, (384, 256, 128, 64, 32, 16)) if has_add else _row_tile(n)
    tn = _pick(nout, (2048, 1280, 1024, 512, 256, 128))
    in_specs = [pl.BlockSpec((tm, k), lambda i, j: (i, xcol))]
    args = [x]
    if has_add:
        nblk = n // tm
        in_specs += [pl.BlockSpec((tm, k), lambda i, j: (i, 0)),
                     pl.BlockSpec((tm, k), lambda i, j: (nblk + i, 0))]
        args += [addends, addends]
    in_specs += [pl.BlockSpec((1, k), lambda i, j: (0, 0)), pl.BlockSpec((k, tn), lambda i, j: (0, j))]
    args += [gain.reshape(1, k).astype(F32), w]
    if has_aux:
        in_specs.append(pl.BlockSpec((k, w_aux.shape[1]), lambda i, j: (0, 0)))
        args.append(w_aux)
    out_shape = [jax.ShapeDtypeStruct((n, nout), BF16)]
    out_specs = [pl.BlockSpec((tm, tn), lambda i, j: (i, j))]
    if has_add:
        out_shape.append(jax.ShapeDtypeStruct((n, k), F32))
        out_specs.append(pl.BlockSpec((tm, k), lambda i, j: (i, 0)))
    if has_aux:
        out_shape.append(jax.ShapeDtypeStruct((n, w_aux.shape[1]), F32))
        out_specs.append(pl.BlockSpec((tm, w_aux.shape[1]), lambda i, j: (i, 0)))
    return pl.pallas_call(
        functools.partial(_norm_proj_kernel, has_add=has_add, has_aux=has_aux),
        grid=(n // tm, nout // tn),
        in_specs=in_specs,
        out_specs=out_specs,
        out_shape=out_shape,
        scratch_shapes=[pltpu.VMEM((tm, k), BF16)],
        compiler_params=_params("parallel", "arbitrary"),
        name=name,
    )(*args)


def _out_proj_kernel(*refs, n_lhs):
    a_refs = refs[:n_lhs]
    w_refs = refs[n_lhs:2 * n_lhs]
    res_ref, o_ref = refs[2 * n_lhs:]
    acc = res_ref[...]
    for a_ref, w_ref in zip(a_refs, w_refs):
        acc = acc + jnp.dot(a_ref[...], w_ref[...], preferred_element_type=F32)
    o_ref[...] = acc


def out_proj(lhs, ws, res, name="out_proj"):
    n, d = res.shape
    tm = _row_tile(n)
    tn = _pick(d, (2048, 1024, 512, 256, 128))
    in_specs = []
    for a in lhs:
        in_specs.append(pl.BlockSpec((tm, a.shape[1]), lambda i, j: (i, 0)))
    for w in ws:
        in_specs.append(pl.BlockSpec((w.shape[0], tn), lambda i, j: (0, j)))
    in_specs.append(pl.BlockSpec((tm, tn), lambda i, j: (i, j)))
    return pl.pallas_call(
        functools.partial(_out_proj_kernel, n_lhs=len(lhs)),
        grid=(n // tm, d // tn),
        in_specs=in_specs,
        out_specs=pl.BlockSpec((tm, tn), lambda i, j: (i, j)),
        out_shape=jax.ShapeDtypeStruct((n, d), F32),
        compiler_params=_params("parallel", "parallel"),
        name=name,
    )(*lhs, *ws, res)


def _router_kernel(x_ref, g_ref, wr_ref, logit_ref):
    xn = _norm_rows(x_ref[...], g_ref[...])
    logit_ref[...] = jnp.dot(xn, wr_ref[...], preferred_element_type=F32,
                             precision=lax.Precision.HIGHEST)


def router_logits(h, gain, router_padded):
    n, d = h.shape
    tm = _row_tile(n)
    ne = router_padded.shape[1]
    return pl.pallas_call(
        _router_kernel,
        grid=(n // tm,),
        in_specs=[pl.BlockSpec((tm, d), lambda i: (i, 0)),
                  pl.BlockSpec((1, d), lambda i: (0, 0)),
                  pl.BlockSpec((d, ne), lambda i: (0, 0))],
        out_specs=pl.BlockSpec((tm, ne), lambda i: (i, 0)),
        out_shape=jax.ShapeDtypeStruct((n, ne), F32),
        compiler_params=_params("parallel"),
        name="router_logits",
    )(h, gain.reshape(1, d).astype(F32), router_padded)


def _final_norm_kernel(x_ref, y0_ref, y1_ref, g_ref, o_ref):
    o_ref[...] = _norm_rows(x_ref[...] + y0_ref[...] + y1_ref[...], g_ref[...])


def final_rmsnorm(h, addends, gain):
    n, d = h.shape
    tm = _row_tile(n)
    nblk = n // tm
    return pl.pallas_call(
        _final_norm_kernel,
        grid=(nblk,),
        in_specs=[pl.BlockSpec((tm, d), lambda i: (i, 0)),
                  pl.BlockSpec((tm, d), lambda i: (i, 0)),
                  pl.BlockSpec((tm, d), lambda i: (nblk + i, 0)),
                  pl.BlockSpec((1, d), lambda i: (0, 0))],
        out_specs=pl.BlockSpec((tm, d), lambda i: (i, 0)),
        out_shape=jax.ShapeDtypeStruct((n, d), F32),
        compiler_params=_params("parallel"),
        name="final_norm",
    )(h, addends, addends, gain.reshape(1, d).astype(F32))


def _swiglu_part(xn, wg, wu, wd):
    g = jnp.dot(xn, wg, preferred_element_type=F32)
    u = jnp.dot(xn, wu, preferred_element_type=F32)
    a = (g * jax.nn.sigmoid(g) * u).astype(BF16)
    return jnp.dot(a, wd, preferred_element_type=F32)


def _dense_ffn_kernel(x_ref, g_ref, wg_ref, wu_ref, wd_ref, o_ref, xn_ref):
    j = pl.program_id(1)

    @pl.when(j == 0)
    def _():
        x = x_ref[...]
        xn_ref[...] = _norm_rows(x, g_ref[...]).astype(BF16)
        o_ref[...] = x

    o_ref[...] += _swiglu_part(xn_ref[...], wg_ref[...], wu_ref[...], wd_ref[...])


def dense_ffn(h, gain, w_gate_up, w_down, layer):
    n, d = h.shape
    dff = w_down.shape[1]
    tm = _row_tile(n)
    tf = _pick(dff, (512, 256, 128))
    nj = dff // tf
    return pl.pallas_call(
        _dense_ffn_kernel,
        grid=(n // tm, nj),
        in_specs=[pl.BlockSpec((tm, d), lambda i, j: (i, 0)),
                  pl.BlockSpec((1, d), lambda i, j: (0, 0)),
                  pl.BlockSpec((None, d, tf), lambda i, j: (layer, 0, j)),
                  pl.BlockSpec((None, d, tf), lambda i, j: (layer, 0, nj + j)),
                  pl.BlockSpec((None, tf, d), lambda i, j: (layer, j, 0))],
        out_specs=pl.BlockSpec((tm, d), lambda i, j: (i, 0)),
        out_shape=jax.ShapeDtypeStruct((n, d), F32),
        scratch_shapes=[pltpu.VMEM((tm, d), BF16)],
        compiler_params=_params("parallel", "arbitrary"),
        name="dense_ffn",
    )(h, gain.reshape(1, d).astype(F32), w_gate_up, w_gate_up, w_down)


def _moe_ffn_kernel(be_ref, nused_ref, tok_ref, dst_ref,
                    h_hbm, gain_ref, gate_ref, wg_ref, wu_ref, wd_ref, y_hbm,
                    xbuf, xn_ref, acc_ref, ybuf, gsem, ssem, *, tm, nj, rows_per_step):
    b = pl.program_id(0)
    j = pl.program_id(1)
    n_used = nused_ref[0]

    def gather_rows(blk, lo, hi):
        slot = blk % 2

        def body(r, c):
            tok = tok_ref[blk * tm + r]
            pltpu.make_async_copy(h_hbm.at[pl.ds(tok, 1)], xbuf.at[slot, pl.ds(r, 1)],
                                  gsem.at[slot]).start()
            return c

        lax.fori_loop(lo, hi, body, 0)

    def scatter_rows(blk, lo, hi):
        def body(r, c):
            dst = dst_ref[blk * tm + r]
            pltpu.make_async_copy(ybuf.at[pl.ds(r, 1)], y_hbm.at[pl.ds(dst, 1)], ssem.at[0]).start()
            return c

        lax.fori_loop(lo, hi, body, 0)

    def wait_scatter():
        pltpu.make_async_copy(ybuf, y_hbm.at[pl.ds(0, tm)], ssem.at[0]).wait()

    def wait_gather(slot):
        pltpu.make_async_copy(h_hbm.at[pl.ds(0, tm)], xbuf.at[slot], gsem.at[slot]).wait()

    trash0 = y_hbm.shape[0] - tm
    nxt = jnp.minimum(b + 1, n_used - 1)
    prv = jnp.maximum(b - 1, 0)

    def move_rows(rows):
        for r in rows:
            pltpu.make_async_copy(h_hbm.at[pl.ds(tok_ref[nxt * tm + r], 1)],
                                  xbuf.at[(b + 1) % 2, pl.ds(r, 1)], gsem.at[(b + 1) % 2]).start()
            dst = jnp.where(b > 0, dst_ref[prv * tm + r], trash0 + r)
            pltpu.make_async_copy(ybuf.at[pl.ds(r, 1)], y_hbm.at[pl.ds(dst, 1)], ssem.at[0]).start()

    @pl.when(b < n_used)
    def _():
        @pl.when(jnp.logical_and(b == 0, j == 0))
        def _():
            gather_rows(0, 0, tm)
            ybuf[...] = jnp.zeros_like(ybuf)
            init = pltpu.make_async_copy(ybuf, y_hbm.at[pl.ds(trash0, tm)], ssem.at[0])
            init.start()
            init.wait()

        @pl.when(j == 0)
        def _():
            wait_gather(b % 2)
            xn_ref[...] = _norm_rows(xbuf[b % 2], gain_ref[...]).astype(BF16)
            acc_ref[...] = jnp.zeros_like(acc_ref)
            move_rows([nj * rows_per_step + r for r in range(tm - nj * rows_per_step)])

        move_rows([j * rows_per_step + k for k in range(rows_per_step)])
        acc_ref[...] += _swiglu_part(xn_ref[...], wg_ref[0], wu_ref[0], wd_ref[0])

        @pl.when(j == nj - 1)
        def _():
            wait_scatter()
            ybuf[...] = acc_ref[...] * gate_ref[...]

            @pl.when(b == n_used - 1)
            def _():
                wait_gather((b + 1) % 2)
                scatter_rows(b, 0, tm)
                wait_scatter()


def moe_ffn(h, gain, blk_e, n_used, row_tok, row_dst, row_gate, w_gate, w_up, w_down, tm):
    n, d = h.shape
    dff = w_down.shape[1]
    tf = _pick(dff, (1024, 512, 256, 128))
    nj = dff // tf
    nb = row_tok.shape[0] // tm
    rows_per_step = tm // nj

    def jj(b, j, nused):
        return jnp.where(b < nused[0], j, nj - 1)

    grid_spec = pltpu.PrefetchScalarGridSpec(
        num_scalar_prefetch=4,
        grid=(nb, nj),
        in_specs=[pl.BlockSpec(memory_space=pl.ANY),
                  pl.BlockSpec((1, d), lambda b, j, be, nu, *_: (0, 0)),
                  pl.BlockSpec((tm, 1), lambda b, j, be, nu, *_: (b, 0)),
                  pl.BlockSpec((1, d, tf), lambda b, j, be, nu, *_: (be[b], 0, jj(b, j, nu))),
                  pl.BlockSpec((1, d, tf), lambda b, j, be, nu, *_: (be[b], 0, jj(b, j, nu))),
                  pl.BlockSpec((1, tf, d), lambda b, j, be, nu, *_: (be[b], jj(b, j, nu), 0))],
        out_specs=pl.BlockSpec(memory_space=pl.ANY),
        scratch_shapes=[pltpu.VMEM((2, tm, d), F32), pltpu.VMEM((tm, d), BF16),
                        pltpu.VMEM((tm, d), F32), pltpu.VMEM((tm, d), F32),
                        pltpu.SemaphoreType.DMA((2,)), pltpu.SemaphoreType.DMA((1,))],
    )
    return pl.pallas_call(
        functools.partial(_moe_ffn_kernel, tm=tm, nj=nj, rows_per_step=rows_per_step),
        grid_spec=grid_spec,
        out_shape=jax.ShapeDtypeStruct((TOP_K * n + tm, d), F32),
        compiler_params=_params("arbitrary", "arbitrary"),
        name="moe_ffn",
    )(blk_e, n_used, row_tok, row_dst, h, gain.reshape(1, d).astype(F32),
      row_gate.reshape(-1, 1), w_gate, w_up, w_down)


def _qk(q, k):
    return lax.dot_general(q, k, (((1,), (1,)), ((), ())), preferred_element_type=F32)


def _pv(p, v):
    return jnp.dot(p.astype(BF16), v, preferred_element_type=F32)


def _flash_update(s, v, m, l, acc):
    m_new = jnp.maximum(m, jnp.max(s, axis=-1, keepdims=True))
    alpha = jnp.exp2(m - m_new)
    p = jnp.exp2(s - m_new)
    l = alpha * l + jnp.sum(p, axis=-1, keepdims=True)
    acc = alpha * acc + _pv(p, v)
    return m_new, l, acc


def _iota2(shape):
    return (lax.broadcasted_iota(jnp.int32, shape, 0), lax.broadcasted_iota(jnp.int32, shape, 1))


def _query_tile(seq):
    return _pick(seq, (512, 256, 128))


def _causal_sweep(heads, nq, tq):
    r16, c16 = _iota2((N_META, BLOCK))
    meta_kv = [(k_ref[0:BLOCK, :], v_ref[0:BLOCK, :]) for _, k_ref, v_ref, _, _ in heads]

    for (q_ref, _, _, o_ref, col_bias), (k0, v0) in zip(heads, meta_kv):
        s = _qk(q_ref[0:N_META, :], k0) + col_bias(None)
        s = jnp.where(c16 <= r16, s, NEG)
        p = jnp.exp2(s - jnp.max(s, axis=-1, keepdims=True))
        o = _pv(p, v0) * (1.0 / jnp.sum(p, axis=-1, keepdims=True))
        o_ref[0:N_META, :] = o.astype(o_ref.dtype)

    def qblock(i, carry):
        qs = pl.multiple_of(N_META + i * tq, N_META)
        cm = lax.broadcasted_iota(jnp.int32, (tq, BLOCK), 1)
        qs_ = []
        state = []
        for (q_ref, _, _, _, col_bias), (k0, v0) in zip(heads, meta_kv):
            q = q_ref[pl.ds(qs, tq), :]
            s = jnp.where(cm < N_META, _qk(q, k0) + col_bias(None), NEG)
            m = jnp.max(s, axis=-1, keepdims=True)
            p = jnp.exp2(s - m)
            qs_.append(q)
            state.append((m, jnp.sum(p, axis=-1, keepdims=True), _pv(p, v0)))

        def chunk(t, state):
            ks = pl.multiple_of(N_META + t * tq, N_META)
            new = []
            for (_, k_ref, v_ref, _, col_bias), q, mla in zip(heads, qs_, state):
                s = _qk(q, k_ref[pl.ds(ks, tq), :]) + col_bias(t)
                new.append(_flash_update(s, v_ref[pl.ds(ks, tq), :], *mla))
            return tuple(new)

        state = lax.fori_loop(0, i, chunk, tuple(state))
        r, c = _iota2((tq, tq))
        for (_, k_ref, v_ref, o_ref, col_bias), q, mla in zip(heads, qs_, state):
            s = _qk(q, k_ref[pl.ds(qs, tq), :]) + col_bias(i)
            s = jnp.where(c <= r, s, NEG)
            m, l, acc = _flash_update(s, v_ref[pl.ds(qs, tq), :], *mla)
            o_ref[pl.ds(qs, tq), :] = (acc * (1.0 / l)).astype(o_ref.dtype)
        return carry

    lax.fori_loop(0, nq, qblock, 0)


def _fox_kernel(q_ref, k_ref, v_ref, cm_ref, cr_ref, o_ref, *, nq, tq):
    def head(g):
        cols = slice(g * FOX_DIM, (g + 1) * FOX_DIM)

        def col_bias(t):
            return -cm_ref[g] if t is None else -cr_ref[g, pl.ds(t, 1), :]

        return (q_ref.at[:, cols], k_ref.at[:, cols], v_ref.at[:, cols], o_ref.at[:, cols], col_bias)

    _causal_sweep([head(g) for g in range(HEADS_PER_STEP)], nq, tq)


def fox_attention(proj, c_meta, c_real, b, l):
    nq, tq = c_real.shape[2:]
    h = FOX_HEADS // HEADS_PER_STEP
    gw = HEADS_PER_STEP * FOX_DIM
    p3 = proj.reshape(b, l, 3 * FOX_HEADS * FOX_DIM)
    blk = lambda off: pl.BlockSpec((None, l, gw), lambda bi, hi: (bi, 0, off + hi))
    out = pl.pallas_call(
        functools.partial(_fox_kernel, nq=nq, tq=tq),
        grid=(b, h),
        in_specs=[blk(0), blk(h), blk(2 * h),
                  pl.BlockSpec((None, HEADS_PER_STEP, 1, BLOCK), lambda bi, hi: (bi, hi, 0, 0)),
                  pl.BlockSpec((None, HEADS_PER_STEP, nq, tq), lambda bi, hi: (bi, hi, 0, 0))],
        out_specs=pl.BlockSpec((None, l, gw), lambda bi, hi: (bi, 0, hi)),
        out_shape=jax.ShapeDtypeStruct((b, l, FOX_HEADS * FOX_DIM), BF16),
        compiler_params=_params("parallel", "parallel"),
        name="fox_attention",
    )(p3, p3, p3, c_meta, c_real)
    return out.reshape(b * l, FOX_HEADS * FOX_DIM)


def _fox_decay_kernel(g_ref, bf_ref, c_ref, *, nchunk):
    r, c = _iota2((BLOCK, BLOCK))
    tri = jnp.where(c <= r, 1.0, 0.0).astype(F32)
    bf = bf_ref[...]

    def log_sigmoid(x):
        return jnp.minimum(x, 0.0) - jnp.log1p(jnp.exp(-jnp.abs(x)))

    def chunk_cumsum(start):
        lf = log_sigmoid(g_ref[pl.ds(start, BLOCK), :] + bf)
        return jnp.dot(tri, lf, preferred_element_type=F32, precision=lax.Precision.HIGHEST)

    head = chunk_cumsum(0)[0:N_META, :]
    c_ref[0:N_META, :] = head * LOG2E

    def body(t, carry):
        start = pl.multiple_of(N_META + t * BLOCK, N_META)
        cc = chunk_cumsum(start) + carry
        c_ref[pl.ds(start, BLOCK), :] = cc * LOG2E
        return cc[BLOCK - 1:BLOCK, :]

    lax.fori_loop(0, nchunk, body, head[N_META - 1:N_META, :])


def fox_decay(gate_logit, b_f_padded, b, l):
    g3 = gate_logit.reshape(b, l, LANES)
    return pl.pallas_call(
        functools.partial(_fox_decay_kernel, nchunk=(l - N_META) // BLOCK),
        grid=(b,),
        in_specs=[pl.BlockSpec((None, l, LANES), lambda bi: (bi, 0, 0)),
                  pl.BlockSpec((1, LANES), lambda bi: (0, 0))],
        out_specs=pl.BlockSpec((None, l, LANES), lambda bi: (bi, 0, 0)),
        out_shape=jax.ShapeDtypeStruct((b, l, LANES), F32),
        compiler_params=_params("parallel"),
        name="fox_decay",
    )(g3, b_f_padded)


def _mla_kernel(q_ref, kv_ref, kr_ref, cs_ref, o_ref, qx_ref, kx_ref, *, nq, tq):
    cs = cs_ref[...]
    lane = lax.broadcasted_iota(jnp.int32, cs.shape, 1)
    ek = kr_ref[...].astype(F32) * cs
    rk = jnp.where(lane < QK_ROPE, ek + pltpu.roll(ek, QK_ROPE, axis=1), 0.0).astype(BF16)
    w = QK_NOPE + 2 * QK_ROPE
    heads = []
    for g in range(HEADS_PER_STEP):
        eq = q_ref[:, g * w + QK_NOPE:(g + 1) * w].astype(F32) * cs
        qx_ref[g, :, :QK_NOPE] = q_ref[:, g * w:g * w + QK_NOPE]
        qx_ref[g, :, QK_NOPE:] = (eq + pltpu.roll(eq, QK_ROPE, axis=1)).astype(BF16)
        kx_ref[g, :, :QK_NOPE] = kv_ref[:, g * w:g * w + QK_NOPE]
        kx_ref[g, :, QK_NOPE:] = rk
        heads.append((qx_ref.at[g], kx_ref.at[g], kv_ref.at[:, g * w + QK_NOPE:(g + 1) * w],
                      o_ref.at[:, g * V_DIM:(g + 1) * V_DIM], lambda t: 0.0))
    _causal_sweep(heads, nq, tq)


def mla_attention(qa, kv, proj, kr_col, cs, b, l):
    tq = _query_tile(l - N_META)
    nq = (l - N_META) // tq
    h = MLA_HEADS // HEADS_PER_STEP
    w = QK_NOPE + 2 * QK_ROPE
    gw = HEADS_PER_STEP * w
    qa3 = qa.reshape(b, l, MLA_HEADS * w)
    kv3 = kv.reshape(b, l, MLA_HEADS * w)
    p3 = proj.reshape(b, l, proj.shape[1])
    out = pl.pallas_call(
        functools.partial(_mla_kernel, nq=nq, tq=tq),
        grid=(b, h),
        in_specs=[pl.BlockSpec((None, l, gw), lambda bi, hi: (bi, 0, hi)),
                  pl.BlockSpec((None, l, gw), lambda bi, hi: (bi, 0, hi)),
                  pl.BlockSpec((None, l, 2 * QK_ROPE), lambda bi, hi: (bi, 0, kr_col)),
                  pl.BlockSpec((l, 2 * QK_ROPE), lambda bi, hi: (0, 0))],
        out_specs=pl.BlockSpec((None, l, HEADS_PER_STEP * V_DIM), lambda bi, hi: (bi, 0, hi)),
        out_shape=jax.ShapeDtypeStruct((b, l, MLA_HEADS * V_DIM), BF16),
        scratch_shapes=[pltpu.VMEM((HEADS_PER_STEP, l, w), BF16),
                        pltpu.VMEM((HEADS_PER_STEP, l, w), BF16)],
        compiler_params=_params("parallel", "parallel"),
        name="mla_attention",
    )(qa3, kv3, p3, cs)
    return out.reshape(b * l, MLA_HEADS * V_DIM)


def _swa_kernel(sink_ref, slope_ref, q_ref, k_ref, v_ref, o_ref, *, nb):
    kvh = pl.program_id(1)
    r16, c16 = _iota2((N_META, BLOCK))
    rel16 = (r16 - c16).astype(F32)
    r3, c3 = _iota2((BLOCK, 3 * BLOCK))
    in_meta = c3 < BLOCK
    in_prev = jnp.logical_and(c3 >= BLOCK, c3 < 2 * BLOCK)
    rel3 = (r3 - c3).astype(F32)
    allowed = jnp.logical_or(
        jnp.logical_or(c3 < N_META, jnp.logical_and(in_prev, c3 - BLOCK > r3)),
        jnp.logical_and(c3 >= 2 * BLOCK, c3 - 2 * BLOCK <= r3))
    mask_bias = jnp.where(allowed, 0.0, NEG)
    prev_bias = jnp.where(in_prev, NEG, 0.0)
    k0 = k_ref[0:BLOCK, :]
    v0 = v_ref[0:BLOCK, :]
    sinks = [sink_ref[kvh * SWA_GROUP + g] for g in range(SWA_GROUP)]
    slopes = [slope_ref[kvh * SWA_GROUP + g] for g in range(SWA_GROUP)]

    for g in range(SWA_GROUP):
        cols = slice(g * SWA_DIM, (g + 1) * SWA_DIM)
        s = _qk(q_ref[0:N_META, cols], k0) - slopes[g] * rel16
        s = jnp.where(c16 <= r16, s, NEG)
        m = jnp.maximum(jnp.max(s, axis=-1, keepdims=True), sinks[g])
        p = jnp.exp2(s - m)
        l = jnp.sum(p, axis=-1, keepdims=True) + jnp.exp2(sinks[g] - m)
        o_ref[0:N_META, cols] = (_pv(p, v0) * (1.0 / l)).astype(o_ref.dtype)

    def qblock(i, carry):
        qs = pl.multiple_of(N_META + i * BLOCK, N_META)
        ps = pl.multiple_of(jnp.where(i > 0, qs - BLOCK, qs), N_META)
        kcat = jnp.concatenate([k0, k_ref[pl.ds(ps, BLOCK), :], k_ref[pl.ds(qs, BLOCK), :]], axis=0)
        vcat = jnp.concatenate([v0, v_ref[pl.ds(ps, BLOCK), :], v_ref[pl.ds(qs, BLOCK), :]], axis=0)
        dist = rel3 + jnp.where(in_meta, jnp.asarray(N_META + i * BLOCK, F32), float(2 * BLOCK))
        bias = mask_bias + prev_bias * jnp.where(i > 0, 0.0, 1.0)
        for g in range(SWA_GROUP):
            cols = slice(g * SWA_DIM, (g + 1) * SWA_DIM)
            s = _qk(q_ref[pl.ds(qs, BLOCK), cols], kcat) + (bias - slopes[g] * dist)
            m = jnp.maximum(jnp.max(s, axis=-1, keepdims=True), sinks[g])
            p = jnp.exp2(s - m)
            l = jnp.sum(p, axis=-1, keepdims=True) + jnp.exp2(sinks[g] - m)
            o_ref[pl.ds(qs, BLOCK), cols] = (_pv(p, vcat) * (1.0 / l)).astype(o_ref.dtype)
        return carry

    lax.fori_loop(0, nb, qblock, 0, unroll=2 if nb % 2 == 0 else 1)


def swa_attention(proj, q_col, k_col, v_col, sinks, slopes, b, l):
    nb = (l - N_META) // BLOCK
    gw = SWA_GROUP * SWA_DIM
    p3 = proj.reshape(b, l, proj.shape[1])
    grid_spec = pltpu.PrefetchScalarGridSpec(
        num_scalar_prefetch=2,
        grid=(b, SWA_KV_HEADS),
        in_specs=[pl.BlockSpec((None, l, gw), lambda bi, hi, s0, s1: (bi, 0, q_col + hi)),
                  pl.BlockSpec((None, l, SWA_DIM), lambda bi, hi, s0, s1: (bi, 0, k_col + hi)),
                  pl.BlockSpec((None, l, SWA_DIM), lambda bi, hi, s0, s1: (bi, 0, v_col + hi))],
        out_specs=pl.BlockSpec((None, l, gw), lambda bi, hi, s0, s1: (bi, 0, hi)),
    )
    out = pl.pallas_call(
        functools.partial(_swa_kernel, nb=nb),
        grid_spec=grid_spec,
        out_shape=jax.ShapeDtypeStruct((b, l, SWA_HEADS * SWA_DIM), BF16),
        compiler_params=_params("parallel", "parallel"),
        name="swa_attention",
    )(sinks, slopes, p3, p3, p3)
    return out.reshape(b * l, SWA_HEADS * SWA_DIM)


def _rope_swap_cols(w):
    half = QK_ROPE // 2
    return jnp.concatenate([-w[..., half:], w[..., :half]], axis=-1)


def _rope_table(l):
    half = QK_ROPE // 2
    inv = ROPE_THETA ** (-jnp.arange(half, dtype=F32) / half)
    ang = jnp.arange(l, dtype=F32)[:, None] * inv[None, :]
    cos, sin = jnp.cos(ang), jnp.sin(ang)
    return jnp.concatenate([cos, cos, sin, sin], axis=-1)


def even_layer(h, pending, b, l, layer, norm1, w_in, q_norm, kv_norm, w_uq, w_ukv, sinks, w_o, norm2,
               w_gate_up, w_down, cs, slopes):
    d = h.shape[1]
    o1, o2, o3 = Q_LORA, Q_LORA + KV_LORA, Q_LORA + KV_LORA + QK_ROPE
    o4 = o3 + SWA_HEADS * SWA_DIM
    o5 = o4 + SWA_KV_HEADS * SWA_DIM
    w_kr = w_in[:, o2:o3]
    w_in2 = jnp.concatenate([w_in[:, o3:o4] * (SWA_DIM ** -0.5 * LOG2E), w_in[:, :o1], w_in[:, o1:o2],
                             w_in[:, o4:o5], w_in[:, o5:], w_kr, _rope_swap_cols(w_kr)],
                            axis=1).astype(BF16)
    w_in2 = jnp.pad(w_in2, ((0, 0), (0, -w_in2.shape[1] % 512)))
    res = norm_proj(h, 0, d, norm1, w_in2, addends=pending, name="even_in_proj")
    proj = res[0]
    if pending is not None:
        h = res[1]
    cq_col = (SWA_HEADS * SWA_DIM) // Q_LORA
    ckv_col = (SWA_HEADS * SWA_DIM + Q_LORA) // KV_LORA
    ks_col = (SWA_HEADS * SWA_DIM + Q_LORA + KV_LORA) // SWA_DIM
    vs_col = ks_col + SWA_KV_HEADS
    kr_col = vs_col + SWA_KV_HEADS

    wq = w_uq.reshape(Q_LORA, MLA_HEADS, QK_NOPE + QK_ROPE) * ((QK_NOPE + QK_ROPE) ** -0.5 * LOG2E)
    wq2 = jnp.concatenate([wq, _rope_swap_cols(wq[..., QK_NOPE:])], axis=-1)
    wq2 = wq2.reshape(Q_LORA, MLA_HEADS * (QK_NOPE + 2 * QK_ROPE)).astype(BF16)
    qa = norm_proj(proj, cq_col, Q_LORA, q_norm, wq2, name="mla_q_up")[0]
    kv = norm_proj(proj, ckv_col, KV_LORA, kv_norm, w_ukv.astype(BF16), name="mla_kv_up")[0]

    o_a = mla_attention(qa, kv, proj, kr_col, cs, b, l)
    o_b = swa_attention(proj, 0, ks_col, vs_col, sinks.astype(F32) * LOG2E, slopes * LOG2E, b, l)
    na = MLA_HEADS * V_DIM
    h = out_proj([o_a, o_b], [w_o[:na].astype(BF16), w_o[na:].astype(BF16)], h, name="even_out_proj")
    return dense_ffn(h, norm2, w_gate_up, w_down, layer)


def moe_layer(h, layer, norm2, router, w_gate, w_up, w_down):
    n, d = h.shape
    ne = router.shape[1]
    router_p = jnp.zeros((d, LANES), F32).at[:, :ne].set(router)
    logits = router_logits(h, norm2, router_p)
    top_val, top_idx = lax.top_k(logits[:, :ne], TOP_K)
    gates = jax.nn.softmax(top_val, axis=-1)
    a = n * TOP_K
    tm = MOE_ROWS if a >= 8 * MOE_ROWS else 64
    flat_e = top_idx.reshape(-1).astype(jnp.int32)
    _, s_pair, s_gate = lax.sort((flat_e, jnp.arange(a, dtype=jnp.int32), gates.reshape(-1)),
                                 num_keys=1, is_stable=True)
    counts = jnp.sum(flat_e[:, None] == jnp.arange(ne, dtype=jnp.int32)[None, :], axis=0)
    padded = (counts + tm - 1) // tm * tm
    start_sorted = jnp.cumsum(counts) - counts
    end_pad = jnp.cumsum(padded)
    start_pad = end_pad - padded
    n_blocks = -(-a // tm) + ne
    p = n_blocks * tm
    blk_start = jnp.arange(n_blocks) * tm
    blk_e = jnp.minimum(jnp.sum(blk_start[:, None] >= end_pad[None, :], axis=1), ne - 1)
    n_used = (end_pad[-1] // tm).astype(jnp.int32).reshape(1)
    in_blk = jnp.arange(tm, dtype=jnp.int32)[None, :]
    rank = (blk_start - start_pad[blk_e]).astype(jnp.int32)[:, None] + in_blk
    valid = (rank < counts[blk_e][:, None]).reshape(p)
    src = jnp.where(valid, (start_sorted[blk_e].astype(jnp.int32)[:, None] + rank).reshape(p), 0)
    pair = s_pair[src]
    trash = jnp.broadcast_to(TOP_K * n + in_blk, (n_blocks, tm)).reshape(p)
    row_tok = jnp.where(valid, pair // TOP_K, 0).astype(jnp.int32)
    row_dst = jnp.where(valid, (pair % TOP_K) * n + pair // TOP_K, trash).astype(jnp.int32)
    row_gate = jnp.where(valid, s_gate[src], 0.0)
    return moe_ffn(h, norm2, (blk_e + layer * ne).astype(jnp.int32), n_used, row_tok, row_dst,
                   row_gate, w_gate, w_up, w_down, tm)


def odd_layer(h, b, l, layer, norm1, w_in, b_f, w_o, norm2, router, w_gate, w_up, w_down):
    d = h.shape[1]
    hd = FOX_HEADS * FOX_DIM
    tq = _query_tile(l - N_META)
    w_qkv = jnp.concatenate([w_in[:, :hd] * (FOX_DIM ** -0.5 * LOG2E), w_in[:, hd:3 * hd]],
                            axis=1).astype(BF16)
    w_f = jnp.zeros((d, LANES), F32).at[:, :FOX_HEADS].set(w_in[:, 3 * hd:]).astype(BF16)
    proj, gate_logit = norm_proj(h, 0, d, norm1, w_qkv, w_aux=w_f, name="fox_in_proj")
    b_f_p = jnp.zeros((1, LANES), F32).at[0, :FOX_HEADS].set(b_f.astype(F32))
    c = fox_decay(gate_logit, b_f_p, b, l)
    c = jnp.swapaxes(c[:, :, :FOX_HEADS], 1, 2)
    c_meta = jnp.zeros((b, FOX_HEADS, 1, BLOCK), F32).at[:, :, 0, :N_META].set(c[:, :, :N_META])
    c_real = c[:, :, N_META:].reshape(b, FOX_HEADS, (l - N_META) // tq, tq)
    o = fox_attention(proj, c_meta, c_real, b, l)
    h = out_proj([o], [w_o.astype(BF16)], h, name="fox_out_proj")
    return h, moe_layer(h, layer, norm2, router, w_gate, w_up, w_down)


def alibi_slopes(n):
    return jnp.asarray([2.0 ** (-8.0 * (i + 1) / n) for i in range(n)], dtype=F32)


def kernel(x, meta, ev_norm1, ev_w_in, ev_q_norm, ev_kv_norm, ev_w_uq, ev_w_ukv, ev_sinks, ev_w_o,
           ev_norm2, ev_w_gate_up, ev_w_down, od_norm1, od_w_in, od_b_f, od_w_o, od_norm2,
           od_router, od_w_gate, od_w_up, od_w_down, final_norm):
    b, seq, d = x.shape
    l = seq + N_META
    depth = ev_norm1.shape[0] + od_norm1.shape[0]
    assert depth % 2 == 0, "the trunk must end with an odd (MoE) layer"
    h = jnp.concatenate([jnp.broadcast_to(meta[None].astype(x.dtype), (b, N_META, d)), x], axis=1)
    h = h.reshape(b * l, d)
    cs = _rope_table(l)
    slopes = alibi_slopes(SWA_HEADS)
    ev_gu, ev_dn = ev_w_gate_up.astype(BF16), ev_w_down.astype(BF16)
    dff = od_w_gate.shape[-1]
    od_g = od_w_gate.astype(BF16).reshape(-1, d, dff)
    od_u = od_w_up.astype(BF16).reshape(-1, d, dff)
    od_d = od_w_down.astype(BF16).reshape(-1, dff, d)
    pending = None
    for layer in range(depth):
        i = layer // 2
        if layer % 2 == 0:
            h = even_layer(h, pending, b, l, i, ev_norm1[i], ev_w_in[i], ev_q_norm[i], ev_kv_norm[i],
                           ev_w_uq[i], ev_w_ukv[i], ev_sinks[i], ev_w_o[i], ev_norm2[i], ev_gu, ev_dn,
                           cs, slopes)
        else:
            h, pending = odd_layer(h, b, l, i, od_norm1[i], od_w_in[i], od_b_f[i], od_w_o[i],
                                   od_norm2[i], od_router[i], od_g, od_u, od_d)
    out = final_rmsnorm(h, pending, final_norm)
    return out.reshape(b, l, d)[:, N_META:]
```

```python
import functools

import jax
import jax.numpy as jnp
from jax import lax
from jax.experimental import pallas as pl
from jax.experimental.pallas import tpu as pltpu

F32 = jnp.float32
BF16 = jnp.bfloat16

N_META = 16
BLOCK = 128
EPS = 1e-6
MLA_HEADS = 8
Q_LORA = 512
KV_LORA = 256
QK_NOPE = 128
QK_ROPE = 64
V_DIM = 128
ROPE_THETA = 10000.0
SWA_HEADS = 8
SWA_KV_HEADS = 2
SWA_DIM = 128
SWA_GROUP = SWA_HEADS // SWA_KV_HEADS
FOX_HEADS = 16
FOX_DIM = 128
TOP_K = 2
MOE_BLOCKS_PER_EXPERT = 8
MOE_LOAD_SLACK = 1.05
BF16_ROWS = 16
HEADS_PER_STEP = 2
LANES = 128
NEG = -1e30
LOG2E = 1.4426950408889634
V7X_VMEM_BYTES = 64 * 1024 * 1024
VMEM_LIMIT = V7X_VMEM_BYTES - 8 * 1024 * 1024


def _params(*sem):
    return pltpu.CompilerParams(dimension_semantics=sem, vmem_limit_bytes=VMEM_LIMIT)


def _pick(n, candidates):
    for c in candidates:
        if n % c == 0:
            return c
    return n


def _row_tile(n):
    return _pick(n, (688, 512, 384, 256, 128, 64, 32, 16))


def _norm_rows(x, g):
    ms = jnp.mean(x * x, axis=-1, keepdims=True)
    return (x * lax.rsqrt(ms + EPS)) * g


def _norm_proj_kernel(*refs, has_add, has_aux):
    refs = list(refs)
    x_ref = refs.pop(0)
    y0_ref, y1_ref = (refs.pop(0), refs.pop(0)) if has_add else (None, None)
    g_ref, w_ref = refs.pop(0), refs.pop(0)
    waux_ref = refs.pop(0) if has_aux else None
    o_ref = refs.pop(0)
    xsum_ref = refs.pop(0) if has_add else None
    oaux_ref = refs.pop(0) if has_aux else None
    xn_ref = refs.pop(0)

    @pl.when(pl.program_id(1) == 0)
    def _():
        x = x_ref[...].astype(F32)
        if has_add:
            x = x + y0_ref[...] + y1_ref[...]
            xsum_ref[...] = x
        xn = _norm_rows(x, g_ref[...]).astype(BF16)
        xn_ref[...] = xn
        if has_aux:
            oaux_ref[...] = jnp.dot(xn, waux_ref[...], preferred_element_type=F32)

    o_ref[...] = jnp.dot(xn_ref[...], w_ref[...], preferred_element_type=F32).astype(o_ref.dtype)


def norm_proj(x, xcol, k, gain, w, addends=None, w_aux=None, name="norm_proj"):
    n = x.shape[0]
    nout = w.shape[1]
    has_add = addends is not None
    has_aux = w_aux is not None
    tm = _pick(n, (384, 256, 128, 64, 32, 16)) if has_add else _row_tile(n)
    tn = _pick(nout, (2048, 1280, 1024, 512, 256, 128))
    in_specs = [pl.BlockSpec((tm, k), lambda i, j: (i, xcol))]
    args = [x]
    if has_add:
        nblk = n // tm
        in_specs += [pl.BlockSpec((tm, k), lambda i, j: (i, 0)),
                     pl.BlockSpec((tm, k), lambda i, j: (nblk + i, 0))]
        args += [addends, addends]
    in_specs += [pl.BlockSpec((1, k), lambda i, j: (0, 0)), pl.BlockSpec((k, tn), lambda i, j: (0, j))]
    args += [gain.reshape(1, k).astype(F32), w]
    if has_aux:
        in_specs.append(pl.BlockSpec((k, w_aux.shape[1]), lambda i, j: (0, 0)))
        args.append(w_aux)
    out_shape = [jax.ShapeDtypeStruct((n, nout), BF16)]
    out_specs = [pl.BlockSpec((tm, tn), lambda i, j: (i, j))]
    if has_add:
        out_shape.append(jax.ShapeDtypeStruct((n, k), F32))
        out_specs.append(pl.BlockSpec((tm, k), lambda i, j: (i, 0)))
    if has_aux:
        out_shape.append(jax.ShapeDtypeStruct((n, w_aux.shape[1]), F32))
        out_specs.append(pl.BlockSpec((tm, w_aux.shape[1]), lambda i, j: (i, 0)))
    return pl.pallas_call(
        functools.partial(_norm_proj_kernel, has_add=has_add, has_aux=has_aux),
        grid=(n // tm, nout // tn),
        in_specs=in_specs,
        out_specs=out_specs,
        out_shape=out_shape,
        scratch_shapes=[pltpu.VMEM((tm, k), BF16)],
        compiler_params=_params("parallel", "arbitrary"),
        name=name,
    )(*args)


def _out_proj_kernel(*refs, n_lhs):
    a_refs = refs[:n_lhs]
    w_refs = refs[n_lhs:2 * n_lhs]
    res_ref, o_ref = refs[2 * n_lhs:]
    acc = res_ref[...]
    for a_ref, w_ref in zip(a_refs, w_refs):
        acc = acc + jnp.dot(a_ref[...], w_ref[...], preferred_element_type=F32)
    o_ref[...] = acc


def out_proj(lhs, ws, res, name="out_proj"):
    n, d = res.shape
    tm = _row_tile(n)
    tn = _pick(d, (2048, 1024, 512, 256, 128))
    in_specs = []
    for a in lhs:
        in_specs.append(pl.BlockSpec((tm, a.shape[1]), lambda i, j: (i, 0)))
    for w in ws:
        in_specs.append(pl.BlockSpec((w.shape[0], tn), lambda i, j: (0, j)))
    in_specs.append(pl.BlockSpec((tm, tn), lambda i, j: (i, j)))
    return pl.pallas_call(
        functools.partial(_out_proj_kernel, n_lhs=len(lhs)),
        grid=(n // tm, d // tn),
        in_specs=in_specs,
        out_specs=pl.BlockSpec((tm, tn), lambda i, j: (i, j)),
        out_shape=jax.ShapeDtypeStruct((n, d), F32),
        compiler_params=_params("parallel", "parallel"),
        name=name,
    )(*lhs, *ws, res)


def _router_kernel(x_ref, g_ref, wr_ref, logit_ref):
    xn = _norm_rows(x_ref[...], g_ref[...])
    logit_ref[...] = jnp.dot(xn, wr_ref[...], preferred_element_type=F32,
                             precision=lax.Precision.HIGHEST)


def router_logits(h, gain, router_padded):
    n, d = h.shape
    tm = _row_tile(n)
    ne = router_padded.shape[1]
    return pl.pallas_call(
        _router_kernel,
        grid=(n // tm,),
        in_specs=[pl.BlockSpec((tm, d), lambda i: (i, 0)),
                  pl.BlockSpec((1, d), lambda i: (0, 0)),
                  pl.BlockSpec((d, ne), lambda i: (0, 0))],
        out_specs=pl.BlockSpec((tm, ne), lambda i: (i, 0)),
        out_shape=jax.ShapeDtypeStruct((n, ne), F32),
        compiler_params=_params("parallel"),
        name="router_logits",
    )(h, gain.reshape(1, d).astype(F32), router_padded)


def _final_norm_kernel(x_ref, y0_ref, y1_ref, g_ref, o_ref):
    o_ref[...] = _norm_rows(x_ref[...] + y0_ref[...] + y1_ref[...], g_ref[...])


def final_rmsnorm(h, addends, gain, b, l):
    n, d = h.shape
    seq = l - N_META
    tr = _pick(seq, (512, 256, 128, 64, 32, 16))

    def rows(offset):
        def index(bi, s):
            return pl.multiple_of(offset + bi * l + N_META + s * tr, N_META), 0
        return pl.BlockSpec((pl.Element(tr), pl.Element(d)), index)

    return pl.pallas_call(
        _final_norm_kernel,
        grid=(b, seq // tr),
        in_specs=[rows(0), rows(0), rows(n), pl.BlockSpec((1, d), lambda bi, s: (0, 0))],
        out_specs=pl.BlockSpec((None, tr, d), lambda bi, s: (bi, s, 0)),
        out_shape=jax.ShapeDtypeStruct((b, seq, d), F32),
        compiler_params=_params("parallel", "parallel"),
        name="final_norm",
    )(h, addends, addends, gain.reshape(1, d).astype(F32))


def _swiglu_part(xn, wg, wu, wd):
    g = jnp.dot(xn, wg, preferred_element_type=F32)
    u = jnp.dot(xn, wu, preferred_element_type=F32)
    a = (g * jax.nn.sigmoid(g) * u).astype(BF16)
    return jnp.dot(a, wd, preferred_element_type=F32)


def _dense_ffn_kernel(x_ref, g_ref, wg_ref, wu_ref, wd_ref, o_ref, xn_ref):
    j = pl.program_id(1)

    @pl.when(j == 0)
    def _():
        x = x_ref[...]
        xn_ref[...] = _norm_rows(x, g_ref[...]).astype(BF16)
        o_ref[...] = x

    o_ref[...] += _swiglu_part(xn_ref[...], wg_ref[...], wu_ref[...], wd_ref[...])


def dense_ffn(h, gain, w_gate_up, w_down, layer):
    n, d = h.shape
    dff = w_down.shape[1]
    tm = _row_tile(n)
    tf = _pick(dff, (512, 256, 128))
    nj = dff // tf
    return pl.pallas_call(
        _dense_ffn_kernel,
        grid=(n // tm, nj),
        in_specs=[pl.BlockSpec((tm, d), lambda i, j: (i, 0)),
                  pl.BlockSpec((1, d), lambda i, j: (0, 0)),
                  pl.BlockSpec((None, d, tf), lambda i, j: (layer, 0, j)),
                  pl.BlockSpec((None, d, tf), lambda i, j: (layer, 0, nj + j)),
                  pl.BlockSpec((None, tf, d), lambda i, j: (layer, j, 0))],
        out_specs=pl.BlockSpec((tm, d), lambda i, j: (i, 0)),
        out_shape=jax.ShapeDtypeStruct((n, d), F32),
        scratch_shapes=[pltpu.VMEM((tm, d), BF16)],
        compiler_params=_params("parallel", "arbitrary"),
        name="dense_ffn",
    )(h, gain.reshape(1, d).astype(F32), w_gate_up, w_gate_up, w_down)


def _moe_ffn_kernel(be_ref, nused_ref, tok_ref, dst_ref,
                    h_hbm, gain_ref, gate_ref, wg_ref, wu_ref, wd_ref, y_hbm,
                    xbuf, xn_ref, acc_ref, ybuf, gsem, ssem, *, tm, nj, rows_per_step):
    b = pl.program_id(0)
    j = pl.program_id(1)
    n_used = nused_ref[0]

    def gather_rows(blk, lo, hi):
        slot = blk % 2

        def body(r, c):
            tok = tok_ref[blk * tm + r]
            pltpu.make_async_copy(h_hbm.at[pl.ds(tok, 1)], xbuf.at[slot, pl.ds(r, 1)],
                                  gsem.at[slot]).start()
            return c

        lax.fori_loop(lo, hi, body, 0)

    def scatter_rows(blk, lo, hi):
        def body(r, c):
            dst = dst_ref[blk * tm + r]
            pltpu.make_async_copy(ybuf.at[pl.ds(r, 1)], y_hbm.at[pl.ds(dst, 1)], ssem.at[0]).start()
            return c

        lax.fori_loop(lo, hi, body, 0)

    def wait_scatter():
        pltpu.make_async_copy(ybuf, y_hbm.at[pl.ds(0, tm)], ssem.at[0]).wait()

    def wait_gather(slot):
        pltpu.make_async_copy(h_hbm.at[pl.ds(0, tm)], xbuf.at[slot], gsem.at[slot]).wait()

    trash0 = y_hbm.shape[0] - tm
    nxt = jnp.minimum(b + 1, n_used - 1)
    prv = jnp.maximum(b - 1, 0)

    def move_rows(rows):
        for r in rows:
            pltpu.make_async_copy(h_hbm.at[pl.ds(tok_ref[nxt * tm + r], 1)],
                                  xbuf.at[(b + 1) % 2, pl.ds(r, 1)], gsem.at[(b + 1) % 2]).start()
            dst = jnp.where(b > 0, dst_ref[prv * tm + r], trash0 + r)
            pltpu.make_async_copy(ybuf.at[pl.ds(r, 1)], y_hbm.at[pl.ds(dst, 1)], ssem.at[0]).start()

    @pl.when(b < n_used)
    def _():
        @pl.when(jnp.logical_and(b == 0, j == 0))
        def _():
            gather_rows(0, 0, tm)
            ybuf[...] = jnp.zeros_like(ybuf)
            init = pltpu.make_async_copy(ybuf, y_hbm.at[pl.ds(trash0, tm)], ssem.at[0])
            init.start()
            init.wait()

        @pl.when(j == 0)
        def _():
            wait_gather(b % 2)
            xn_ref[...] = _norm_rows(xbuf[b % 2], gain_ref[...]).astype(BF16)
            acc_ref[...] = jnp.zeros_like(acc_ref)
            move_rows([nj * rows_per_step + r for r in range(tm - nj * rows_per_step)])

        move_rows([j * rows_per_step + k for k in range(rows_per_step)])
        acc_ref[...] += _swiglu_part(xn_ref[...], wg_ref[0], wu_ref[0], wd_ref[0])

        @pl.when(j == nj - 1)
        def _():
            wait_scatter()
            ybuf[...] = acc_ref[...] * gate_ref[...]

            @pl.when(b == n_used - 1)
            def _():
                wait_gather((b + 1) % 2)
                scatter_rows(b, 0, tm)
                wait_scatter()


def moe_ffn(h, gain, blk_e, n_used, row_tok, row_dst, row_gate, w_gate, w_up, w_down, tm):
    n, d = h.shape
    dff = w_down.shape[1]
    tf = _pick(dff, (1024, 512, 256, 128))
    nj = dff // tf
    nb = row_tok.shape[0] // tm
    rows_per_step = tm // nj

    def jj(b, j, nused):
        return jnp.where(b < nused[0], j, nj - 1)

    grid_spec = pltpu.PrefetchScalarGridSpec(
        num_scalar_prefetch=4,
        grid=(nb, nj),
        in_specs=[pl.BlockSpec(memory_space=pl.ANY),
                  pl.BlockSpec((1, d), lambda b, j, be, nu, *_: (0, 0)),
                  pl.BlockSpec((tm, 1), lambda b, j, be, nu, *_: (b, 0)),
                  pl.BlockSpec((1, d, tf), lambda b, j, be, nu, *_: (be[b], 0, jj(b, j, nu))),
                  pl.BlockSpec((1, d, tf), lambda b, j, be, nu, *_: (be[b], 0, jj(b, j, nu))),
                  pl.BlockSpec((1, tf, d), lambda b, j, be, nu, *_: (be[b], jj(b, j, nu), 0))],
        out_specs=pl.BlockSpec(memory_space=pl.ANY),
        scratch_shapes=[pltpu.VMEM((2, tm, d), F32), pltpu.VMEM((tm, d), BF16),
                        pltpu.VMEM((tm, d), F32), pltpu.VMEM((tm, d), F32),
                        pltpu.SemaphoreType.DMA((2,)), pltpu.SemaphoreType.DMA((1,))],
    )
    return pl.pallas_call(
        functools.partial(_moe_ffn_kernel, tm=tm, nj=nj, rows_per_step=rows_per_step),
        grid_spec=grid_spec,
        out_shape=jax.ShapeDtypeStruct((TOP_K * n + tm, d), F32),
        compiler_params=_params("arbitrary", "arbitrary"),
        name="moe_ffn",
    )(blk_e, n_used, row_tok, row_dst, h, gain.reshape(1, d).astype(F32),
      row_gate.reshape(-1, 1), w_gate, w_up, w_down)


def _qk(q, k):
    return lax.dot_general(q, k, (((1,), (1,)), ((), ())), preferred_element_type=F32)


def _pv(p, v):
    return jnp.dot(p.astype(BF16), v, preferred_element_type=F32)


def _flash_update(s, v, m, l, acc):
    m_new = jnp.maximum(m, jnp.max(s, axis=-1, keepdims=True))
    alpha = jnp.exp2(m - m_new)
    p = jnp.exp2(s - m_new)
    l = alpha * l + jnp.sum(p, axis=-1, keepdims=True)
    acc = alpha * acc + _pv(p, v)
    return m_new, l, acc


def _iota2(shape):
    return (lax.broadcasted_iota(jnp.int32, shape, 0), lax.broadcasted_iota(jnp.int32, shape, 1))


def _query_tile(seq):
    return _pick(seq, (512, 256, 128))


def _causal_sweep(heads, nq, tq):
    r16, c16 = _iota2((N_META, BLOCK))
    meta_kv = [(k_ref[0:BLOCK, :], v_ref[0:BLOCK, :]) for _, k_ref, v_ref, _, _ in heads]

    for (q_ref, _, _, o_ref, col_bias), (k0, v0) in zip(heads, meta_kv):
        s = _qk(q_ref[0:N_META, :], k0) + col_bias(None)
        s = jnp.where(c16 <= r16, s, NEG)
        p = jnp.exp2(s - jnp.max(s, axis=-1, keepdims=True))
        o = _pv(p, v0) * (1.0 / jnp.sum(p, axis=-1, keepdims=True))
        o_ref[0:N_META, :] = o.astype(o_ref.dtype)

    def qblock(i, carry):
        qs = pl.multiple_of(N_META + i * tq, N_META)
        cm = lax.broadcasted_iota(jnp.int32, (tq, BLOCK), 1)
        r, c = _iota2((tq, tq))
        qs_ = []
        state = []
        for (q_ref, k_ref, v_ref, _, col_bias), (k0, v0) in zip(heads, meta_kv):
            q = q_ref[pl.ds(qs, tq), :]
            s0 = jnp.where(cm < N_META, _qk(q, k0) + col_bias(None), NEG)
            s1 = jnp.where(c <= r, _qk(q, k_ref[pl.ds(qs, tq), :]) + col_bias(i), NEG)
            m = jnp.maximum(jnp.max(s0, axis=-1, keepdims=True), jnp.max(s1, axis=-1, keepdims=True))
            p0 = jnp.exp2(s0 - m)
            p1 = jnp.exp2(s1 - m)
            l = jnp.sum(p0, axis=-1, keepdims=True) + jnp.sum(p1, axis=-1, keepdims=True)
            qs_.append(q)
            state.append((m, l, _pv(p0, v0) + _pv(p1, v_ref[pl.ds(qs, tq), :])))

        def chunk(t, state):
            ks = pl.multiple_of(N_META + t * tq, N_META)
            new = []
            for (_, k_ref, v_ref, _, col_bias), q, mla in zip(heads, qs_, state):
                s = _qk(q, k_ref[pl.ds(ks, tq), :]) + col_bias(t)
                new.append(_flash_update(s, v_ref[pl.ds(ks, tq), :], *mla))
            return tuple(new)

        state = lax.fori_loop(0, i, chunk, tuple(state))
        for (_, _, _, o_ref, _), (m, l, acc) in zip(heads, state):
            o_ref[pl.ds(qs, tq), :] = (acc * (1.0 / l)).astype(o_ref.dtype)
        return carry

    lax.fori_loop(0, nq, qblock, 0)


def _fox_kernel(q_ref, k_ref, v_ref, cm_ref, cr_ref, o_ref, *, nq, tq):
    def head(g):
        cols = slice(g * FOX_DIM, (g + 1) * FOX_DIM)

        def col_bias(t):
            return -cm_ref[g] if t is None else -cr_ref[g, pl.ds(t, 1), :]

        return (q_ref.at[:, cols], k_ref.at[:, cols], v_ref.at[:, cols], o_ref.at[:, cols], col_bias)

    _causal_sweep([head(g) for g in range(HEADS_PER_STEP)], nq, tq)


def fox_attention(proj, c_meta, c_real, b, l):
    nq, tq = c_real.shape[2:]
    h = FOX_HEADS // HEADS_PER_STEP
    gw = HEADS_PER_STEP * FOX_DIM
    p3 = proj.reshape(b, l, 3 * FOX_HEADS * FOX_DIM)
    blk = lambda off: pl.BlockSpec((None, l, gw), lambda bi, hi: (bi, 0, off + hi))
    out = pl.pallas_call(
        functools.partial(_fox_kernel, nq=nq, tq=tq),
        grid=(b, h),
        in_specs=[blk(0), blk(h), blk(2 * h),
                  pl.BlockSpec((None, HEADS_PER_STEP, 1, BLOCK), lambda bi, hi: (bi, hi, 0, 0)),
                  pl.BlockSpec((None, HEADS_PER_STEP, nq, tq), lambda bi, hi: (bi, hi, 0, 0))],
        out_specs=pl.BlockSpec((None, l, gw), lambda bi, hi: (bi, 0, hi)),
        out_shape=jax.ShapeDtypeStruct((b, l, FOX_HEADS * FOX_DIM), BF16),
        compiler_params=_params("parallel", "parallel"),
        name="fox_attention",
    )(p3, p3, p3, c_meta, c_real)
    return out.reshape(b * l, FOX_HEADS * FOX_DIM)


def _fox_decay_kernel(g_ref, bf_ref, c_ref, *, nchunk):
    r, c = _iota2((BLOCK, BLOCK))
    tri = jnp.where(c <= r, 1.0, 0.0).astype(F32)
    bf = bf_ref[...]

    def log_sigmoid(x):
        return jnp.minimum(x, 0.0) - jnp.log1p(jnp.exp(-jnp.abs(x)))

    def chunk_cumsum(start):
        lf = log_sigmoid(g_ref[pl.ds(start, BLOCK), :] + bf)
        return jnp.dot(tri, lf, preferred_element_type=F32, precision=lax.Precision.HIGHEST)

    head = chunk_cumsum(0)[0:N_META, :]
    c_ref[0:N_META, :] = head * LOG2E

    def body(t, carry):
        start = pl.multiple_of(N_META + t * BLOCK, N_META)
        cc = chunk_cumsum(start) + carry
        c_ref[pl.ds(start, BLOCK), :] = cc * LOG2E
        return cc[BLOCK - 1:BLOCK, :]

    lax.fori_loop(0, nchunk, body, head[N_META - 1:N_META, :])


def fox_decay(gate_logit, b_f_padded, b, l):
    g3 = gate_logit.reshape(b, l, LANES)
    return pl.pallas_call(
        functools.partial(_fox_decay_kernel, nchunk=(l - N_META) // BLOCK),
        grid=(b,),
        in_specs=[pl.BlockSpec((None, l, LANES), lambda bi: (bi, 0, 0)),
                  pl.BlockSpec((1, LANES), lambda bi: (0, 0))],
        out_specs=pl.BlockSpec((None, l, LANES), lambda bi: (bi, 0, 0)),
        out_shape=jax.ShapeDtypeStruct((b, l, LANES), F32),
        compiler_params=_params("parallel"),
        name="fox_decay",
    )(g3, b_f_padded)


def _mla_kernel(q_ref, kv_ref, kr_ref, cs_ref, o_ref, qx_ref, kx_ref, *, nq, tq):
    cs = cs_ref[...]
    lane = lax.broadcasted_iota(jnp.int32, cs.shape, 1)
    ek = kr_ref[...].astype(F32) * cs
    rk = jnp.where(lane < QK_ROPE, ek + pltpu.roll(ek, QK_ROPE, axis=1), 0.0).astype(BF16)
    w = QK_NOPE + 2 * QK_ROPE
    heads = []
    for g in range(HEADS_PER_STEP):
        eq = q_ref[:, g * w + QK_NOPE:(g + 1) * w].astype(F32) * cs
        qx_ref[g, :, :QK_NOPE] = q_ref[:, g * w:g * w + QK_NOPE]
        qx_ref[g, :, QK_NOPE:] = (eq + pltpu.roll(eq, QK_ROPE, axis=1)).astype(BF16)
        kx_ref[g, :, :QK_NOPE] = kv_ref[:, g * w:g * w + QK_NOPE]
        kx_ref[g, :, QK_NOPE:] = rk
        heads.append((qx_ref.at[g], kx_ref.at[g], kv_ref.at[:, g * w + QK_NOPE:(g + 1) * w],
                      o_ref.at[:, g * V_DIM:(g + 1) * V_DIM], lambda t: 0.0))
    _causal_sweep(heads, nq, tq)


def mla_attention(qa, kv, proj, kr_col, cs, b, l):
    tq = _query_tile(l - N_META)
    nq = (l - N_META) // tq
    h = MLA_HEADS // HEADS_PER_STEP
    w = QK_NOPE + 2 * QK_ROPE
    gw = HEADS_PER_STEP * w
    qa3 = qa.reshape(b, l, MLA_HEADS * w)
    kv3 = kv.reshape(b, l, MLA_HEADS * w)
    p3 = proj.reshape(b, l, proj.shape[1])
    out = pl.pallas_call(
        functools.partial(_mla_kernel, nq=nq, tq=tq),
        grid=(b, h),
        in_specs=[pl.BlockSpec((None, l, gw), lambda bi, hi: (bi, 0, hi)),
                  pl.BlockSpec((None, l, gw), lambda bi, hi: (bi, 0, hi)),
                  pl.BlockSpec((None, l, 2 * QK_ROPE), lambda bi, hi: (bi, 0, kr_col)),
                  pl.BlockSpec((l, 2 * QK_ROPE), lambda bi, hi: (0, 0))],
        out_specs=pl.BlockSpec((None, l, HEADS_PER_STEP * V_DIM), lambda bi, hi: (bi, 0, hi)),
        out_shape=jax.ShapeDtypeStruct((b, l, MLA_HEADS * V_DIM), BF16),
        scratch_shapes=[pltpu.VMEM((HEADS_PER_STEP, l, w), BF16),
                        pltpu.VMEM((HEADS_PER_STEP, l, w), BF16)],
        compiler_params=_params("parallel", "parallel"),
        name="mla_attention",
    )(qa3, kv3, p3, cs)
    return out.reshape(b * l, MLA_HEADS * V_DIM)


def _swa_kernel(sink_ref, slope_ref, q_ref, k_ref, v_ref, o_ref, *, nb):
    kvh = pl.program_id(1)
    r16, c16 = _iota2((N_META, BLOCK))
    rel16 = (r16 - c16).astype(F32)
    r3, c3 = _iota2((BLOCK, 3 * BLOCK))
    in_meta = c3 < BLOCK
    in_prev = jnp.logical_and(c3 >= BLOCK, c3 < 2 * BLOCK)
    rel3 = (r3 - c3).astype(F32)
    allowed = jnp.logical_or(
        jnp.logical_or(c3 < N_META, jnp.logical_and(in_prev, c3 - BLOCK > r3)),
        jnp.logical_and(c3 >= 2 * BLOCK, c3 - 2 * BLOCK <= r3))
    mask_bias = jnp.where(allowed, 0.0, NEG)
    prev_bias = jnp.where(in_prev, NEG, 0.0)
    k0 = k_ref[0:BLOCK, :]
    v0 = v_ref[0:BLOCK, :]
    sinks = [sink_ref[kvh * SWA_GROUP + g] for g in range(SWA_GROUP)]
    slopes = [slope_ref[kvh * SWA_GROUP + g] for g in range(SWA_GROUP)]

    for g in range(SWA_GROUP):
        cols = slice(g * SWA_DIM, (g + 1) * SWA_DIM)
        s = _qk(q_ref[0:N_META, cols], k0) - slopes[g] * rel16
        s = jnp.where(c16 <= r16, s, NEG)
        m = jnp.maximum(jnp.max(s, axis=-1, keepdims=True), sinks[g])
        p = jnp.exp2(s - m)
        l = jnp.sum(p, axis=-1, keepdims=True) + jnp.exp2(sinks[g] - m)
        o_ref[0:N_META, cols] = (_pv(p, v0) * (1.0 / l)).astype(o_ref.dtype)

    def qblock(i, carry):
        qs = pl.multiple_of(N_META + i * BLOCK, N_META)
        ps = pl.multiple_of(jnp.where(i > 0, qs - BLOCK, qs), N_META)
        kcat = jnp.concatenate([k0, k_ref[pl.ds(ps, BLOCK), :], k_ref[pl.ds(qs, BLOCK), :]], axis=0)
        vcat = jnp.concatenate([v0, v_ref[pl.ds(ps, BLOCK), :], v_ref[pl.ds(qs, BLOCK), :]], axis=0)
        dist = rel3 + jnp.where(in_meta, jnp.asarray(N_META + i * BLOCK, F32), float(2 * BLOCK))
        bias = mask_bias + prev_bias * jnp.where(i > 0, 0.0, 1.0)
        for g in range(SWA_GROUP):
            cols = slice(g * SWA_DIM, (g + 1) * SWA_DIM)
            s = _qk(q_ref[pl.ds(qs, BLOCK), cols], kcat) + (bias - slopes[g] * dist)
            m = jnp.maximum(jnp.max(s, axis=-1, keepdims=True), sinks[g])
            p = jnp.exp2(s - m)
            l = jnp.sum(p, axis=-1, keepdims=True) + jnp.exp2(sinks[g] - m)
            o_ref[pl.ds(qs, BLOCK), cols] = (_pv(p, vcat) * (1.0 / l)).astype(o_ref.dtype)
        return carry

    lax.fori_loop(0, nb, qblock, 0, unroll=2 if nb % 2 == 0 else 1)


def swa_attention(proj, q_col, k_col, v_col, sinks, slopes, b, l):
    nb = (l - N_META) // BLOCK
    gw = SWA_GROUP * SWA_DIM
    p3 = proj.reshape(b, l, proj.shape[1])
    grid_spec = pltpu.PrefetchScalarGridSpec(
        num_scalar_prefetch=2,
        grid=(b, SWA_KV_HEADS),
        in_specs=[pl.BlockSpec((None, l, gw), lambda bi, hi, s0, s1: (bi, 0, q_col + hi)),
                  pl.BlockSpec((None, l, SWA_DIM), lambda bi, hi, s0, s1: (bi, 0, k_col + hi)),
                  pl.BlockSpec((None, l, SWA_DIM), lambda bi, hi, s0, s1: (bi, 0, v_col + hi))],
        out_specs=pl.BlockSpec((None, l, gw), lambda bi, hi, s0, s1: (bi, 0, hi)),
    )
    out = pl.pallas_call(
        functools.partial(_swa_kernel, nb=nb),
        grid_spec=grid_spec,
        out_shape=jax.ShapeDtypeStruct((b, l, SWA_HEADS * SWA_DIM), BF16),
        compiler_params=_params("parallel", "parallel"),
        name="swa_attention",
    )(sinks, slopes, p3, p3, p3)
    return out.reshape(b * l, SWA_HEADS * SWA_DIM)


def _rope_swap_cols(w):
    half = QK_ROPE // 2
    return jnp.concatenate([-w[..., half:], w[..., :half]], axis=-1)


def _rope_table(l):
    half = QK_ROPE // 2
    inv = ROPE_THETA ** (-jnp.arange(half, dtype=F32) / half)
    ang = jnp.arange(l, dtype=F32)[:, None] * inv[None, :]
    cos, sin = jnp.cos(ang), jnp.sin(ang)
    return jnp.concatenate([cos, cos, sin, sin], axis=-1)


def even_layer(h, pending, b, l, layer, norm1, w_in, q_norm, kv_norm, w_uq, w_ukv, sinks, w_o, norm2,
               w_gate_up, w_down, cs, slopes):
    d = h.shape[1]
    o1, o2, o3 = Q_LORA, Q_LORA + KV_LORA, Q_LORA + KV_LORA + QK_ROPE
    o4 = o3 + SWA_HEADS * SWA_DIM
    o5 = o4 + SWA_KV_HEADS * SWA_DIM
    w_kr = w_in[:, o2:o3]
    w_in2 = jnp.concatenate([w_in[:, o3:o4] * (SWA_DIM ** -0.5 * LOG2E), w_in[:, :o1], w_in[:, o1:o2],
                             w_in[:, o4:o5], w_in[:, o5:], w_kr, _rope_swap_cols(w_kr)],
                            axis=1).astype(BF16)
    w_in2 = jnp.pad(w_in2, ((0, 0), (0, -w_in2.shape[1] % 512)))
    res = norm_proj(h, 0, d, norm1, w_in2, addends=pending, name="even_in_proj")
    proj = res[0]
    if pending is not None:
        h = res[1]
    cq_col = (SWA_HEADS * SWA_DIM) // Q_LORA
    ckv_col = (SWA_HEADS * SWA_DIM + Q_LORA) // KV_LORA
    ks_col = (SWA_HEADS * SWA_DIM + Q_LORA + KV_LORA) // SWA_DIM
    vs_col = ks_col + SWA_KV_HEADS
    kr_col = vs_col + SWA_KV_HEADS

    wq = w_uq.reshape(Q_LORA, MLA_HEADS, QK_NOPE + QK_ROPE) * ((QK_NOPE + QK_ROPE) ** -0.5 * LOG2E)
    wq2 = jnp.concatenate([wq, _rope_swap_cols(wq[..., QK_NOPE:])], axis=-1)
    wq2 = wq2.reshape(Q_LORA, MLA_HEADS * (QK_NOPE + 2 * QK_ROPE)).astype(BF16)
    qa = norm_proj(proj, cq_col, Q_LORA, q_norm, wq2, name="mla_q_up")[0]
    kv = norm_proj(proj, ckv_col, KV_LORA, kv_norm, w_ukv.astype(BF16), name="mla_kv_up")[0]

    o_a = mla_attention(qa, kv, proj, kr_col, cs, b, l)
    o_b = swa_attention(proj, 0, ks_col, vs_col, sinks.astype(F32) * LOG2E, slopes * LOG2E, b, l)
    na = MLA_HEADS * V_DIM
    h = out_proj([o_a, o_b], [w_o[:na].astype(BF16), w_o[na:].astype(BF16)], h, name="even_out_proj")
    return dense_ffn(h, norm2, w_gate_up, w_down, layer)


def moe_layer(h, layer, norm2, router, w_gate, w_up, w_down):
    n, d = h.shape
    ne = router.shape[1]
    router_p = jnp.zeros((d, LANES), F32).at[:, :ne].set(router)
    logits = router_logits(h, norm2, router_p)
    top_val, top_idx = lax.top_k(logits[:, :ne], TOP_K)
    gates = jax.nn.softmax(top_val, axis=-1)
    a = n * TOP_K
    share = -(-int(a / ne * MOE_LOAD_SLACK) // MOE_BLOCKS_PER_EXPERT)
    tm = -(-share // BF16_ROWS) * BF16_ROWS
    flat_e = top_idx.reshape(-1).astype(jnp.int32)
    _, s_pair, s_gate = lax.sort((flat_e, jnp.arange(a, dtype=jnp.int32), gates.reshape(-1)),
                                 num_keys=1, is_stable=True)
    counts = jnp.sum(flat_e[:, None] == jnp.arange(ne, dtype=jnp.int32)[None, :], axis=0)
    padded = (counts + tm - 1) // tm * tm
    start_sorted = jnp.cumsum(counts) - counts
    end_pad = jnp.cumsum(padded)
    start_pad = end_pad - padded
    n_blocks = -(-a // tm) + ne
    p = n_blocks * tm
    blk_start = jnp.arange(n_blocks) * tm
    blk_e = jnp.minimum(jnp.sum(blk_start[:, None] >= end_pad[None, :], axis=1), ne - 1)
    n_used = (end_pad[-1] // tm).astype(jnp.int32).reshape(1)
    in_blk = jnp.arange(tm, dtype=jnp.int32)[None, :]
    rank = (blk_start - start_pad[blk_e]).astype(jnp.int32)[:, None] + in_blk
    valid = (rank < counts[blk_e][:, None]).reshape(p)
    src = jnp.where(valid, (start_sorted[blk_e].astype(jnp.int32)[:, None] + rank).reshape(p), 0)
    pair = s_pair[src]
    trash = jnp.broadcast_to(TOP_K * n + in_blk, (n_blocks, tm)).reshape(p)
    row_tok = jnp.where(valid, pair // TOP_K, 0).astype(jnp.int32)
    row_dst = jnp.where(valid, (pair % TOP_K) * n + pair // TOP_K, trash).astype(jnp.int32)
    row_gate = jnp.where(valid, s_gate[src], 0.0)
    return moe_ffn(h, norm2, (blk_e + layer * ne).astype(jnp.int32), n_used, row_tok, row_dst,
                   row_gate, w_gate, w_up, w_down, tm)


def odd_layer(h, b, l, layer, norm1, w_in, b_f, w_o, norm2, router, w_gate, w_up, w_down):
    d = h.shape[1]
    hd = FOX_HEADS * FOX_DIM
    tq = _query_tile(l - N_META)
    w_qkv = jnp.concatenate([w_in[:, :hd] * (FOX_DIM ** -0.5 * LOG2E), w_in[:, hd:3 * hd]],
                            axis=1).astype(BF16)
    w_f = jnp.zeros((d, LANES), F32).at[:, :FOX_HEADS].set(w_in[:, 3 * hd:]).astype(BF16)
    proj, gate_logit = norm_proj(h, 0, d, norm1, w_qkv, w_aux=w_f, name="fox_in_proj")
    b_f_p = jnp.zeros((1, LANES), F32).at[0, :FOX_HEADS].set(b_f.astype(F32))
    c = fox_decay(gate_logit, b_f_p, b, l)
    c = jnp.swapaxes(c[:, :, :FOX_HEADS], 1, 2)
    c_meta = jnp.zeros((b, FOX_HEADS, 1, BLOCK), F32).at[:, :, 0, :N_META].set(c[:, :, :N_META])
    c_real = c[:, :, N_META:].reshape(b, FOX_HEADS, (l - N_META) // tq, tq)
    o = fox_attention(proj, c_meta, c_real, b, l)
    h = out_proj([o], [w_o.astype(BF16)], h, name="fox_out_proj")
    return h, moe_layer(h, layer, norm2, router, w_gate, w_up, w_down)


def alibi_slopes(n):
    return jnp.asarray([2.0 ** (-8.0 * (i + 1) / n) for i in range(n)], dtype=F32)


def kernel(x, meta, ev_norm1, ev_w_in, ev_q_norm, ev_kv_norm, ev_w_uq, ev_w_ukv, ev_sinks, ev_w_o,
           ev_norm2, ev_w_gate_up, ev_w_down, od_norm1, od_w_in, od_b_f, od_w_o, od_norm2,
           od_router, od_w_gate, od_w_up, od_w_down, final_norm):
    b, seq, d = x.shape
    l = seq + N_META
    depth = ev_norm1.shape[0] + od_norm1.shape[0]
    assert depth % 2 == 0, "the trunk must end with an odd (MoE) layer"
    h = jnp.concatenate([jnp.broadcast_to(meta[None].astype(x.dtype), (b, N_META, d)), x], axis=1)
    h = h.reshape(b * l, d)
    cs = _rope_table(l)
    slopes = alibi_slopes(SWA_HEADS)
    ev_gu, ev_dn = ev_w_gate_up.astype(BF16), ev_w_down.astype(BF16)
    dff = od_w_gate.shape[-1]
    od_g = od_w_gate.astype(BF16).reshape(-1, d, dff)
    od_u = od_w_up.astype(BF16).reshape(-1, d, dff)
    od_d = od_w_down.astype(BF16).reshape(-1, dff, d)
    pending = None
    for layer in range(depth):
        i = layer // 2
        if layer % 2 == 0:
            h = even_layer(h, pending, b, l, i, ev_norm1[i], ev_w_in[i], ev_q_norm[i], ev_kv_norm[i],
                           ev_w_uq[i], ev_w_ukv[i], ev_sinks[i], ev_w_o[i], ev_norm2[i], ev_gu, ev_dn,
                           cs, slopes)
        else:
            h, pending = odd_layer(h, b, l, i, od_norm1[i], od_w_in[i], od_b_f[i], od_w_o[i],
                                   od_norm2[i], od_router[i], od_g, od_u, od_d)
    return final_rmsnorm(h, pending, final_norm, b, l)
```

```python
import functools

import jax
import jax.numpy as jnp
from jax import lax
from jax.experimental import pallas as pl
from jax.experimental.pallas import tpu as pltpu

F32 = jnp.float32
BF16 = jnp.bfloat16

N_META = 16
BLOCK = 128
EPS = 1e-6
MLA_HEADS = 8
Q_LORA = 512
KV_LORA = 256
QK_NOPE = 128
QK_ROPE = 64
V_DIM = 128
ROPE_THETA = 10000.0
SWA_HEADS = 8
SWA_KV_HEADS = 2
SWA_DIM = 128
SWA_GROUP = SWA_HEADS // SWA_KV_HEADS
FOX_HEADS = 16
FOX_DIM = 128
TOP_K = 2
MOE_ROWS = 512
BF16_ROWS = 16
HEADS_PER_STEP = 4
LANES = 128
NEG = -1e30
LOG2E = 1.4426950408889634
V7X_VMEM_BYTES = 64 * 1024 * 1024
VMEM_RESERVE = 8 * 1024 * 1024
VMEM_LIMIT = V7X_VMEM_BYTES - VMEM_RESERVE


def _params(*sem):
    return pltpu.CompilerParams(dimension_semantics=sem, vmem_limit_bytes=VMEM_LIMIT)


def _pick(n, candidates):
    for c in candidates:
        if n % c == 0:
            return c
    return n


def _row_tile(n):
    return _pick(n, (688, 512, 384, 256, 128, 64, 32, 16))


def _norm_rows(x, g):
    ms = jnp.mean(x * x, axis=-1, keepdims=True)
    return (x * lax.rsqrt(ms + EPS)) * g


def _norm_proj_kernel(*refs, has_add, has_aux):
    refs = list(refs)
    x_ref = refs.pop(0)
    y0_ref, y1_ref = (refs.pop(0), refs.pop(0)) if has_add else (None, None)
    g_ref, w_ref = refs.pop(0), refs.pop(0)
    waux_ref = refs.pop(0) if has_aux else None
    o_ref = refs.pop(0)
    xsum_ref = refs.pop(0) if has_add else None
    oaux_ref = refs.pop(0) if has_aux else None
    xn_ref = refs.pop(0)

    @pl.when(pl.program_id(1) == 0)
    def _():
        x = x_ref[...].astype(F32)
        if has_add:
            x = x + y0_ref[...] + y1_ref[...]
            xsum_ref[...] = x
        xn = _norm_rows(x, g_ref[...]).astype(BF16)
        xn_ref[...] = xn
        if has_aux:
            oaux_ref[...] = jnp.dot(xn, waux_ref[...], preferred_element_type=F32)

    o_ref[...] = jnp.dot(xn_ref[...], w_ref[...], preferred_element_type=F32).astype(o_ref.dtype)


def norm_proj(x, xcol, k, gain, w, addends=None, w_aux=None, name="norm_proj"):
    n = x.shape[0]
    nout = w.shape[1]
    has_add = addends is not None
    has_aux = w_aux is not None
    tm = _pick(n, (384, 256, 128, 64, 32, 16)) if has_add else _row_tile(n)
    tn = _pick(nout, (2048, 1280, 1024, 512, 256, 128))
    in_specs = [pl.BlockSpec((tm, k), lambda i, j: (i, xcol))]
    args = [x]
    if has_add:
        nblk = n // tm
        in_specs += [pl.BlockSpec((tm, k), lambda i, j: (i, 0)),
                     pl.BlockSpec((tm, k), lambda i, j: (nblk + i, 0))]
        args += [addends, addends]
    in_specs += [pl.BlockSpec((1, k), lambda i, j: (0, 0)), pl.BlockSpec((k, tn), lambda i, j: (0, j))]
    args += [gain.reshape(1, k).astype(F32), w]
    if has_aux:
        in_specs.append(pl.BlockSpec((k, w_aux.shape[1]), lambda i, j: (0, 0)))
        args.append(w_aux)
    out_shape = [jax.ShapeDtypeStruct((n, nout), BF16)]
    out_specs = [pl.BlockSpec((tm, tn), lambda i, j: (i, j))]
    if has_add:
        out_shape.append(jax.ShapeDtypeStruct((n, k), F32))
        out_specs.append(pl.BlockSpec((tm, k), lambda i, j: (i, 0)))
    if has_aux:
        out_shape.append(jax.ShapeDtypeStruct((n, w_aux.shape[1]), F32))
        out_specs.append(pl.BlockSpec((tm, w_aux.shape[1]), lambda i, j: (i, 0)))
    return pl.pallas_call(
        functools.partial(_norm_proj_kernel, has_add=has_add, has_aux=has_aux),
        grid=(n // tm, nout // tn),
        in_specs=in_specs,
        out_specs=out_specs,
        out_shape=out_shape,
        scratch_shapes=[pltpu.VMEM((tm, k), BF16)],
        compiler_params=_params("parallel", "arbitrary"),
        name=name,
    )(*args)


def _out_proj_kernel(*refs, n_lhs):
    a_refs = refs[:n_lhs]
    w_refs = refs[n_lhs:2 * n_lhs]
    res_ref, o_ref = refs[2 * n_lhs:]
    acc = res_ref[...]
    for a_ref, w_ref in zip(a_refs, w_refs):
        acc = acc + jnp.dot(a_ref[...], w_ref[...], preferred_element_type=F32)
    o_ref[...] = acc


def out_proj(lhs, ws, res, name="out_proj"):
    n, d = res.shape
    tm = _row_tile(n)
    tn = _pick(d, (2048, 1024, 512, 256, 128))
    in_specs = []
    for a in lhs:
        in_specs.append(pl.BlockSpec((tm, a.shape[1]), lambda i, j: (i, 0)))
    for w in ws:
        in_specs.append(pl.BlockSpec((w.shape[0], tn), lambda i, j: (0, j)))
    in_specs.append(pl.BlockSpec((tm, tn), lambda i, j: (i, j)))
    return pl.pallas_call(
        functools.partial(_out_proj_kernel, n_lhs=len(lhs)),
        grid=(n // tm, d // tn),
        in_specs=in_specs,
        out_specs=pl.BlockSpec((tm, tn), lambda i, j: (i, j)),
        out_shape=jax.ShapeDtypeStruct((n, d), F32),
        compiler_params=_params("parallel", "parallel"),
        name=name,
    )(*lhs, *ws, res)


def _router_kernel(x_ref, g_ref, wr_ref, logit_ref):
    xn = _norm_rows(x_ref[...], g_ref[...])
    logit_ref[...] = jnp.dot(xn, wr_ref[...], preferred_element_type=F32,
                             precision=lax.Precision.HIGHEST)


def router_logits(h, gain, router_padded):
    n, d = h.shape
    tm = _row_tile(n)
    ne = router_padded.shape[1]
    return pl.pallas_call(
        _router_kernel,
        grid=(n // tm,),
        in_specs=[pl.BlockSpec((tm, d), lambda i: (i, 0)),
                  pl.BlockSpec((1, d), lambda i: (0, 0)),
                  pl.BlockSpec((d, ne), lambda i: (0, 0))],
        out_specs=pl.BlockSpec((tm, ne), lambda i: (i, 0)),
        out_shape=jax.ShapeDtypeStruct((n, ne), F32),
        compiler_params=_params("parallel"),
        name="router_logits",
    )(h, gain.reshape(1, d).astype(F32), router_padded)


def _final_norm_kernel(x_ref, y0_ref, y1_ref, g_ref, o_ref):
    o_ref[...] = _norm_rows(x_ref[...] + y0_ref[...] + y1_ref[...], g_ref[...])


def final_rmsnorm(h, addends, gain, b, l):
    n, d = h.shape
    seq = l - N_META
    tr = _pick(seq, (512, 256, 128, 64, 32, 16))

    def rows(offset):
        def index(bi, s):
            return pl.multiple_of(offset + bi * l + N_META + s * tr, N_META), 0
        return pl.BlockSpec((pl.Element(tr), pl.Element(d)), index)

    return pl.pallas_call(
        _final_norm_kernel,
        grid=(b, seq // tr),
        in_specs=[rows(0), rows(0), rows(n), pl.BlockSpec((1, d), lambda bi, s: (0, 0))],
        out_specs=pl.BlockSpec((None, tr, d), lambda bi, s: (bi, s, 0)),
        out_shape=jax.ShapeDtypeStruct((b, seq, d), F32),
        compiler_params=_params("parallel", "parallel"),
        name="final_norm",
    )(h, addends, addends, gain.reshape(1, d).astype(F32))


def _swiglu_part(xn, wg, wu, wd):
    g = jnp.dot(xn, wg, preferred_element_type=F32)
    u = jnp.dot(xn, wu, preferred_element_type=F32)
    a = (g * jax.nn.sigmoid(g) * u).astype(BF16)
    return jnp.dot(a, wd, preferred_element_type=F32)


def _dense_ffn_kernel(x_ref, g_ref, wg_ref, wu_ref, wd_ref, o_ref, xn_ref):
    j = pl.program_id(1)

    @pl.when(j == 0)
    def _():
        x = x_ref[...]
        xn_ref[...] = _norm_rows(x, g_ref[...]).astype(BF16)
        o_ref[...] = x

    o_ref[...] += _swiglu_part(xn_ref[...], wg_ref[...], wu_ref[...], wd_ref[...])


def dense_ffn(h, gain, w_gate_up, w_down, layer):
    n, d = h.shape
    dff = w_down.shape[1]
    tm = _row_tile(n)
    tf = _pick(dff, (512, 256, 128))
    nj = dff // tf
    return pl.pallas_call(
        _dense_ffn_kernel,
        grid=(n // tm, nj),
        in_specs=[pl.BlockSpec((tm, d), lambda i, j: (i, 0)),
                  pl.BlockSpec((1, d), lambda i, j: (0, 0)),
                  pl.BlockSpec((None, d, tf), lambda i, j: (layer, 0, j)),
                  pl.BlockSpec((None, d, tf), lambda i, j: (layer, 0, nj + j)),
                  pl.BlockSpec((None, tf, d), lambda i, j: (layer, j, 0))],
        out_specs=pl.BlockSpec((tm, d), lambda i, j: (i, 0)),
        out_shape=jax.ShapeDtypeStruct((n, d), F32),
        scratch_shapes=[pltpu.VMEM((tm, d), BF16)],
        compiler_params=_params("parallel", "arbitrary"),
        name="dense_ffn",
    )(h, gain.reshape(1, d).astype(F32), w_gate_up, w_gate_up, w_down)


def _moe_ffn_kernel(be_ref, nused_ref, tok_ref, dst_ref,
                    h_hbm, gain_ref, gate_ref, wg_ref, wu_ref, wd_ref, y_hbm,
                    xbuf, xn_ref, acc_ref, ybuf, gsem, ssem, *, tm, nj, rows_per_step):
    b = pl.program_id(0)
    j = pl.program_id(1)
    n_used = nused_ref[0]

    def gather_rows(blk, lo, hi):
        slot = blk % 2

        def body(r, c):
            tok = tok_ref[blk * tm + r]
            pltpu.make_async_copy(h_hbm.at[pl.ds(tok, 1)], xbuf.at[slot, pl.ds(r, 1)],
                                  gsem.at[slot]).start()
            return c

        lax.fori_loop(lo, hi, body, 0)

    def scatter_rows(blk, lo, hi):
        def body(r, c):
            dst = dst_ref[blk * tm + r]
            pltpu.make_async_copy(ybuf.at[pl.ds(r, 1)], y_hbm.at[pl.ds(dst, 1)], ssem.at[0]).start()
            return c

        lax.fori_loop(lo, hi, body, 0)

    def wait_scatter():
        pltpu.make_async_copy(ybuf, y_hbm.at[pl.ds(0, tm)], ssem.at[0]).wait()

    def wait_gather(slot):
        pltpu.make_async_copy(h_hbm.at[pl.ds(0, tm)], xbuf.at[slot], gsem.at[slot]).wait()

    trash0 = y_hbm.shape[0] - tm
    nxt = jnp.minimum(b + 1, n_used - 1)
    prv = jnp.maximum(b - 1, 0)

    def move_rows(rows):
        for r in rows:
            pltpu.make_async_copy(h_hbm.at[pl.ds(tok_ref[nxt * tm + r], 1)],
                                  xbuf.at[(b + 1) % 2, pl.ds(r, 1)], gsem.at[(b + 1) % 2]).start()
            dst = jnp.where(b > 0, dst_ref[prv * tm + r], trash0 + r)
            pltpu.make_async_copy(ybuf.at[pl.ds(r, 1)], y_hbm.at[pl.ds(dst, 1)], ssem.at[0]).start()

    @pl.when(b < n_used)
    def _():
        @pl.when(jnp.logical_and(b == 0, j == 0))
        def _():
            gather_rows(0, 0, tm)
            ybuf[...] = jnp.zeros_like(ybuf)
            init = pltpu.make_async_copy(ybuf, y_hbm.at[pl.ds(trash0, tm)], ssem.at[0])
            init.start()
            init.wait()

        @pl.when(j == 0)
        def _():
            wait_gather(b % 2)
            xn_ref[...] = _norm_rows(xbuf[b % 2], gain_ref[...]).astype(BF16)
            acc_ref[...] = jnp.zeros_like(acc_ref)
            move_rows([nj * rows_per_step + r for r in range(tm - nj * rows_per_step)])

        move_rows([j * rows_per_step + k for k in range(rows_per_step)])
        acc_ref[...] += _swiglu_part(xn_ref[...], wg_ref[0], wu_ref[0], wd_ref[0])

        @pl.when(j == nj - 1)
        def _():
            wait_scatter()
            ybuf[...] = acc_ref[...] * gate_ref[...]

            @pl.when(b == n_used - 1)
            def _():
                wait_gather((b + 1) % 2)
                scatter_rows(b, 0, tm)
                wait_scatter()


def moe_ffn(h, gain, blk_e, n_used, row_tok, row_dst, row_gate, w_gate, w_up, w_down, tm):
    n, d = h.shape
    dff = w_down.shape[1]
    tf = _pick(dff, (1024, 512, 256, 128))
    nj = dff // tf
    nb = row_tok.shape[0] // tm
    rows_per_step = tm // nj

    def jj(b, j, nused):
        return jnp.where(b < nused[0], j, nj - 1)

    grid_spec = pltpu.PrefetchScalarGridSpec(
        num_scalar_prefetch=4,
        grid=(nb, nj),
        in_specs=[pl.BlockSpec(memory_space=pl.ANY),
                  pl.BlockSpec((1, d), lambda b, j, be, nu, *_: (0, 0)),
                  pl.BlockSpec((tm, 1), lambda b, j, be, nu, *_: (b, 0)),
                  pl.BlockSpec((1, d, tf), lambda b, j, be, nu, *_: (be[b], 0, jj(b, j, nu))),
                  pl.BlockSpec((1, d, tf), lambda b, j, be, nu, *_: (be[b], 0, jj(b, j, nu))),
                  pl.BlockSpec((1, tf, d), lambda b, j, be, nu, *_: (be[b], jj(b, j, nu), 0))],
        out_specs=pl.BlockSpec(memory_space=pl.ANY),
        scratch_shapes=[pltpu.VMEM((2, tm, d), F32), pltpu.VMEM((tm, d), BF16),
                        pltpu.VMEM((tm, d), F32), pltpu.VMEM((tm, d), F32),
                        pltpu.SemaphoreType.DMA((2,)), pltpu.SemaphoreType.DMA((1,))],
    )
    return pl.pallas_call(
        functools.partial(_moe_ffn_kernel, tm=tm, nj=nj, rows_per_step=rows_per_step),
        grid_spec=grid_spec,
        out_shape=jax.ShapeDtypeStruct((TOP_K * n + tm, d), F32),
        compiler_params=_params("arbitrary", "arbitrary"),
        name="moe_ffn",
    )(blk_e, n_used, row_tok, row_dst, h, gain.reshape(1, d).astype(F32),
      row_gate.reshape(-1, 1), w_gate, w_up, w_down)


def _qk(q, k):
    return lax.dot_general(q, k, (((1,), (1,)), ((), ())), preferred_element_type=F32)


def _pv(p, v):
    return jnp.dot(p.astype(BF16), v, preferred_element_type=F32)


def _flash_update(s, v, m, l, acc):
    m_new = jnp.maximum(m, jnp.max(s, axis=-1, keepdims=True))
    alpha = jnp.exp2(m - m_new)
    p = jnp.exp2(s - m_new)
    l = alpha * l + jnp.sum(p, axis=-1, keepdims=True)
    acc = alpha * acc + _pv(p, v)
    return m_new, l, acc


def _iota2(shape):
    return (lax.broadcasted_iota(jnp.int32, shape, 0), lax.broadcasted_iota(jnp.int32, shape, 1))


def _query_tile(seq):
    return _pick(seq, (512, 256, 128))


def _causal_sweep(heads, nq, tq):
    r16, c16 = _iota2((N_META, BLOCK))
    meta_kv = [(k_ref[0:BLOCK, :], v_ref[0:BLOCK, :]) for _, k_ref, v_ref, _, _ in heads]

    for (q_ref, _, _, o_ref, col_bias), (k0, v0) in zip(heads, meta_kv):
        s = _qk(q_ref[0:N_META, :], k0) + col_bias(None)
        s = jnp.where(c16 <= r16, s, NEG)
        p = jnp.exp2(s - jnp.max(s, axis=-1, keepdims=True))
        o = _pv(p, v0) * (1.0 / jnp.sum(p, axis=-1, keepdims=True))
        o_ref[0:N_META, :] = o.astype(o_ref.dtype)

    def qblock(i, carry):
        qs = pl.multiple_of(N_META + i * tq, N_META)
        cm = lax.broadcasted_iota(jnp.int32, (tq, BLOCK), 1)
        r, c = _iota2((tq, tq))
        qs_ = []
        state = []
        for (q_ref, k_ref, v_ref, _, col_bias), (k0, v0) in zip(heads, meta_kv):
            q = q_ref[pl.ds(qs, tq), :]
            s0 = jnp.where(cm < N_META, _qk(q, k0) + col_bias(None), NEG)
            s1 = jnp.where(c <= r, _qk(q, k_ref[pl.ds(qs, tq), :]) + col_bias(i), NEG)
            m = jnp.maximum(jnp.max(s0, axis=-1, keepdims=True), jnp.max(s1, axis=-1, keepdims=True))
            p0 = jnp.exp2(s0 - m)
            p1 = jnp.exp2(s1 - m)
            l = jnp.sum(p0, axis=-1, keepdims=True) + jnp.sum(p1, axis=-1, keepdims=True)
            qs_.append(q)
            state.append((m, l, _pv(p0, v0) + _pv(p1, v_ref[pl.ds(qs, tq), :])))

        def chunk(t, state):
            ks = pl.multiple_of(N_META + t * tq, N_META)
            new = []
            for (_, k_ref, v_ref, _, col_bias), q, mla in zip(heads, qs_, state):
                s = _qk(q, k_ref[pl.ds(ks, tq), :]) + col_bias(t)
                new.append(_flash_update(s, v_ref[pl.ds(ks, tq), :], *mla))
            return tuple(new)

        state = lax.fori_loop(0, i, chunk, tuple(state))
        for (_, _, _, o_ref, _), (m, l, acc) in zip(heads, state):
            o_ref[pl.ds(qs, tq), :] = (acc * (1.0 / l)).astype(o_ref.dtype)
        return carry

    lax.fori_loop(0, nq, qblock, 0)


def _fox_kernel(q_ref, k_ref, v_ref, cm_ref, cr_ref, o_ref, *, nq, tq):
    def head(g):
        cols = slice(g * FOX_DIM, (g + 1) * FOX_DIM)

        def col_bias(t):
            return -cm_ref[g] if t is None else -cr_ref[g, pl.ds(t, 1), :]

        return (q_ref.at[:, cols], k_ref.at[:, cols], v_ref.at[:, cols], o_ref.at[:, cols], col_bias)

    _causal_sweep([head(g) for g in range(HEADS_PER_STEP)], nq, tq)


def fox_attention(proj, c_meta, c_real, b, l):
    nq, tq = c_real.shape[2:]
    h = FOX_HEADS // HEADS_PER_STEP
    gw = HEADS_PER_STEP * FOX_DIM
    p3 = proj.reshape(b, l, 3 * FOX_HEADS * FOX_DIM)
    blk = lambda off: pl.BlockSpec((None, l, gw), lambda bi, hi: (bi, 0, off + hi))
    out = pl.pallas_call(
        functools.partial(_fox_kernel, nq=nq, tq=tq),
        grid=(b, h),
        in_specs=[blk(0), blk(h), blk(2 * h),
                  pl.BlockSpec((None, HEADS_PER_STEP, 1, BLOCK), lambda bi, hi: (bi, hi, 0, 0)),
                  pl.BlockSpec((None, HEADS_PER_STEP, nq, tq), lambda bi, hi: (bi, hi, 0, 0))],
        out_specs=pl.BlockSpec((None, l, gw), lambda bi, hi: (bi, 0, hi)),
        out_shape=jax.ShapeDtypeStruct((b, l, FOX_HEADS * FOX_DIM), BF16),
        compiler_params=_params("parallel", "parallel"),
        name="fox_attention",
    )(p3, p3, p3, c_meta, c_real)
    return out.reshape(b * l, FOX_HEADS * FOX_DIM)


def _fox_decay_kernel(g_ref, bf_ref, c_ref, *, nchunk):
    r, c = _iota2((BLOCK, BLOCK))
    tri = jnp.where(c <= r, 1.0, 0.0).astype(F32)
    bf = bf_ref[...]

    def log_sigmoid(x):
        return jnp.minimum(x, 0.0) - jnp.log1p(jnp.exp(-jnp.abs(x)))

    def chunk_cumsum(start):
        lf = log_sigmoid(g_ref[pl.ds(start, BLOCK), :] + bf)
        return jnp.dot(tri, lf, preferred_element_type=F32, precision=lax.Precision.HIGHEST)

    head = chunk_cumsum(0)[0:N_META, :]
    c_ref[0:N_META, :] = head * LOG2E

    def body(t, carry):
        start = pl.multiple_of(N_META + t * BLOCK, N_META)
        cc = chunk_cumsum(start) + carry
        c_ref[pl.ds(start, BLOCK), :] = cc * LOG2E
        return cc[BLOCK - 1:BLOCK, :]

    lax.fori_loop(0, nchunk, body, head[N_META - 1:N_META, :])


def fox_decay(gate_logit, b_f_padded, b, l):
    g3 = gate_logit.reshape(b, l, LANES)
    return pl.pallas_call(
        functools.partial(_fox_decay_kernel, nchunk=(l - N_META) // BLOCK),
        grid=(b,),
        in_specs=[pl.BlockSpec((None, l, LANES), lambda bi: (bi, 0, 0)),
                  pl.BlockSpec((1, LANES), lambda bi: (0, 0))],
        out_specs=pl.BlockSpec((None, l, LANES), lambda bi: (bi, 0, 0)),
        out_shape=jax.ShapeDtypeStruct((b, l, LANES), F32),
        compiler_params=_params("parallel"),
        name="fox_decay",
    )(g3, b_f_padded)


def _mla_kernel(q_ref, kv_ref, kr_ref, cs_ref, o_ref, qx_ref, kx_ref, *, nq, tq):
    cs = cs_ref[...]
    lane = lax.broadcasted_iota(jnp.int32, cs.shape, 1)
    ek = kr_ref[...].astype(F32) * cs
    rk = jnp.where(lane < QK_ROPE, ek + pltpu.roll(ek, QK_ROPE, axis=1), 0.0).astype(BF16)
    w = QK_NOPE + 2 * QK_ROPE
    heads = []
    for g in range(HEADS_PER_STEP):
        eq = q_ref[:, g * w + QK_NOPE:(g + 1) * w].astype(F32) * cs
        qx_ref[g, :, :QK_NOPE] = q_ref[:, g * w:g * w + QK_NOPE]
        qx_ref[g, :, QK_NOPE:] = (eq + pltpu.roll(eq, QK_ROPE, axis=1)).astype(BF16)
        kx_ref[g, :, :QK_NOPE] = kv_ref[:, g * w:g * w + QK_NOPE]
        kx_ref[g, :, QK_NOPE:] = rk
        heads.append((qx_ref.at[g], kx_ref.at[g], kv_ref.at[:, g * w + QK_NOPE:(g + 1) * w],
                      o_ref.at[:, g * V_DIM:(g + 1) * V_DIM], lambda t: 0.0))
    _causal_sweep(heads, nq, tq)


def mla_attention(qa, kv, proj, kr_col, cs, b, l):
    tq = _query_tile(l - N_META)
    nq = (l - N_META) // tq
    h = MLA_HEADS // HEADS_PER_STEP
    w = QK_NOPE + 2 * QK_ROPE
    gw = HEADS_PER_STEP * w
    qa3 = qa.reshape(b, l, MLA_HEADS * w)
    kv3 = kv.reshape(b, l, MLA_HEADS * w)
    p3 = proj.reshape(b, l, proj.shape[1])
    out = pl.pallas_call(
        functools.partial(_mla_kernel, nq=nq, tq=tq),
        grid=(b, h),
        in_specs=[pl.BlockSpec((None, l, gw), lambda bi, hi: (bi, 0, hi)),
                  pl.BlockSpec((None, l, gw), lambda bi, hi: (bi, 0, hi)),
                  pl.BlockSpec((None, l, 2 * QK_ROPE), lambda bi, hi: (bi, 0, kr_col)),
                  pl.BlockSpec((l, 2 * QK_ROPE), lambda bi, hi: (0, 0))],
        out_specs=pl.BlockSpec((None, l, HEADS_PER_STEP * V_DIM), lambda bi, hi: (bi, 0, hi)),
        out_shape=jax.ShapeDtypeStruct((b, l, MLA_HEADS * V_DIM), BF16),
        scratch_shapes=[pltpu.VMEM((HEADS_PER_STEP, l, w), BF16),
                        pltpu.VMEM((HEADS_PER_STEP, l, w), BF16)],
        compiler_params=_params("parallel", "parallel"),
        name="mla_attention",
    )(qa3, kv3, p3, cs)
    return out.reshape(b * l, MLA_HEADS * V_DIM)


def _swa_kernel(sink_ref, slope_ref, q_ref, k_ref, v_ref, o_ref, *, nb):
    kvh = pl.program_id(1)
    r16, c16 = _iota2((N_META, BLOCK))
    rel16 = (r16 - c16).astype(F32)
    r3, c3 = _iota2((BLOCK, 3 * BLOCK))
    in_meta = c3 < BLOCK
    in_prev = jnp.logical_and(c3 >= BLOCK, c3 < 2 * BLOCK)
    rel3 = (r3 - c3).astype(F32)
    allowed = jnp.logical_or(
        jnp.logical_or(c3 < N_META, jnp.logical_and(in_prev, c3 - BLOCK > r3)),
        jnp.logical_and(c3 >= 2 * BLOCK, c3 - 2 * BLOCK <= r3))
    mask_bias = jnp.where(allowed, 0.0, NEG)
    prev_bias = jnp.where(in_prev, NEG, 0.0)
    k0 = k_ref[0:BLOCK, :]
    v0 = v_ref[0:BLOCK, :]
    sinks = [sink_ref[kvh * SWA_GROUP + g] for g in range(SWA_GROUP)]
    slopes = [slope_ref[kvh * SWA_GROUP + g] for g in range(SWA_GROUP)]

    for g in range(SWA_GROUP):
        cols = slice(g * SWA_DIM, (g + 1) * SWA_DIM)
        s = _qk(q_ref[0:N_META, cols], k0) - slopes[g] * rel16
        s = jnp.where(c16 <= r16, s, NEG)
        m = jnp.maximum(jnp.max(s, axis=-1, keepdims=True), sinks[g])
        p = jnp.exp2(s - m)
        l = jnp.sum(p, axis=-1, keepdims=True) + jnp.exp2(sinks[g] - m)
        o_ref[0:N_META, cols] = (_pv(p, v0) * (1.0 / l)).astype(o_ref.dtype)

    def qblock(i, carry):
        qs = pl.multiple_of(N_META + i * BLOCK, N_META)
        ps = pl.multiple_of(jnp.where(i > 0, qs - BLOCK, qs), N_META)
        kcat = jnp.concatenate([k0, k_ref[pl.ds(ps, BLOCK), :], k_ref[pl.ds(qs, BLOCK), :]], axis=0)
        vcat = jnp.concatenate([v0, v_ref[pl.ds(ps, BLOCK), :], v_ref[pl.ds(qs, BLOCK), :]], axis=0)
        dist = rel3 + jnp.where(in_meta, jnp.asarray(N_META + i * BLOCK, F32), float(2 * BLOCK))
        bias = mask_bias + prev_bias * jnp.where(i > 0, 0.0, 1.0)
        for g in range(SWA_GROUP):
            cols = slice(g * SWA_DIM, (g + 1) * SWA_DIM)
            s = _qk(q_ref[pl.ds(qs, BLOCK), cols], kcat) + (bias - slopes[g] * dist)
            m = jnp.maximum(jnp.max(s, axis=-1, keepdims=True), sinks[g])
            p = jnp.exp2(s - m)
            l = jnp.sum(p, axis=-1, keepdims=True) + jnp.exp2(sinks[g] - m)
            o_ref[pl.ds(qs, BLOCK), cols] = (_pv(p, vcat) * (1.0 / l)).astype(o_ref.dtype)
        return carry

    lax.fori_loop(0, nb, qblock, 0, unroll=2 if nb % 2 == 0 else 1)


def swa_attention(proj, q_col, k_col, v_col, sinks, slopes, b, l):
    nb = (l - N_META) // BLOCK
    gw = SWA_GROUP * SWA_DIM
    p3 = proj.reshape(b, l, proj.shape[1])
    grid_spec = pltpu.PrefetchScalarGridSpec(
        num_scalar_prefetch=2,
        grid=(b, SWA_KV_HEADS),
        in_specs=[pl.BlockSpec((None, l, gw), lambda bi, hi, s0, s1: (bi, 0, q_col + hi)),
                  pl.BlockSpec((None, l, SWA_DIM), lambda bi, hi, s0, s1: (bi, 0, k_col + hi)),
                  pl.BlockSpec((None, l, SWA_DIM), lambda bi, hi, s0, s1: (bi, 0, v_col + hi))],
        out_specs=pl.BlockSpec((None, l, gw), lambda bi, hi, s0, s1: (bi, 0, hi)),
    )
    out = pl.pallas_call(
        functools.partial(_swa_kernel, nb=nb),
        grid_spec=grid_spec,
        out_shape=jax.ShapeDtypeStruct((b, l, SWA_HEADS * SWA_DIM), BF16),
        compiler_params=_params("parallel", "parallel"),
        name="swa_attention",
    )(sinks, slopes, p3, p3, p3)
    return out.reshape(b * l, SWA_HEADS * SWA_DIM)


def _rope_swap_cols(w):
    half = QK_ROPE // 2
    return jnp.concatenate([-w[..., half:], w[..., :half]], axis=-1)


def _rope_table(l):
    half = QK_ROPE // 2
    inv = ROPE_THETA ** (-jnp.arange(half, dtype=F32) / half)
    ang = jnp.arange(l, dtype=F32)[:, None] * inv[None, :]
    cos, sin = jnp.cos(ang), jnp.sin(ang)
    return jnp.concatenate([cos, cos, sin, sin], axis=-1)


def even_layer(h, pending, b, l, layer, norm1, w_in, q_norm, kv_norm, w_uq, w_ukv, sinks, w_o, norm2,
               w_gate_up, w_down, cs, slopes):
    d = h.shape[1]
    o1, o2, o3 = Q_LORA, Q_LORA + KV_LORA, Q_LORA + KV_LORA + QK_ROPE
    o4 = o3 + SWA_HEADS * SWA_DIM
    o5 = o4 + SWA_KV_HEADS * SWA_DIM
    w_kr = w_in[:, o2:o3]
    w_in2 = jnp.concatenate([w_in[:, o3:o4] * (SWA_DIM ** -0.5 * LOG2E), w_in[:, :o1], w_in[:, o1:o2],
                             w_in[:, o4:o5], w_in[:, o5:], w_kr, _rope_swap_cols(w_kr)],
                            axis=1).astype(BF16)
    w_in2 = jnp.pad(w_in2, ((0, 0), (0, -w_in2.shape[1] % 512)))
    res = norm_proj(h, 0, d, norm1, w_in2, addends=pending, name="even_in_proj")
    proj = res[0]
    if pending is not None:
        h = res[1]
    cq_col = (SWA_HEADS * SWA_DIM) // Q_LORA
    ckv_col = (SWA_HEADS * SWA_DIM + Q_LORA) // KV_LORA
    ks_col = (SWA_HEADS * SWA_DIM + Q_LORA + KV_LORA) // SWA_DIM
    vs_col = ks_col + SWA_KV_HEADS
    kr_col = vs_col + SWA_KV_HEADS

    wq = w_uq.reshape(Q_LORA, MLA_HEADS, QK_NOPE + QK_ROPE) * ((QK_NOPE + QK_ROPE) ** -0.5 * LOG2E)
    wq2 = jnp.concatenate([wq, _rope_swap_cols(wq[..., QK_NOPE:])], axis=-1)
    wq2 = wq2.reshape(Q_LORA, MLA_HEADS * (QK_NOPE + 2 * QK_ROPE)).astype(BF16)
    qa = norm_proj(proj, cq_col, Q_LORA, q_norm, wq2, name="mla_q_up")[0]
    kv = norm_proj(proj, ckv_col, KV_LORA, kv_norm, w_ukv.astype(BF16), name="mla_kv_up")[0]

    o_a = mla_attention(qa, kv, proj, kr_col, cs, b, l)
    o_b = swa_attention(proj, 0, ks_col, vs_col, sinks.astype(F32) * LOG2E, slopes * LOG2E, b, l)
    na = MLA_HEADS * V_DIM
    h = out_proj([o_a, o_b], [w_o[:na].astype(BF16), w_o[na:].astype(BF16)], h, name="even_out_proj")
    return dense_ffn(h, norm2, w_gate_up, w_down, layer)


def moe_layer(h, layer, norm2, router, w_gate, w_up, w_down):
    n, d = h.shape
    ne = router.shape[1]
    router_p = jnp.zeros((d, LANES), F32).at[:, :ne].set(router)
    logits = router_logits(h, norm2, router_p)
    top_val, top_idx = lax.top_k(logits[:, :ne], TOP_K)
    gates = jax.nn.softmax(top_val, axis=-1)
    a = n * TOP_K
    tm = min(MOE_ROWS, -(-(a // (8 * ne)) // BF16_ROWS) * BF16_ROWS)
    flat_e = top_idx.reshape(-1).astype(jnp.int32)
    _, s_pair, s_gate = lax.sort((flat_e, jnp.arange(a, dtype=jnp.int32), gates.reshape(-1)),
                                 num_keys=1, is_stable=True)
    counts = jnp.sum(flat_e[:, None] == jnp.arange(ne, dtype=jnp.int32)[None, :], axis=0)
    padded = (counts + tm - 1) // tm * tm
    start_sorted = jnp.cumsum(counts) - counts
    end_pad = jnp.cumsum(padded)
    start_pad = end_pad - padded
    n_blocks = -(-a // tm) + ne
    p = n_blocks * tm
    blk_start = jnp.arange(n_blocks) * tm
    blk_e = jnp.minimum(jnp.sum(blk_start[:, None] >= end_pad[None, :], axis=1), ne - 1)
    n_used = (end_pad[-1] // tm).astype(jnp.int32).reshape(1)
    in_blk = jnp.arange(tm, dtype=jnp.int32)[None, :]
    rank = (blk_start - start_pad[blk_e]).astype(jnp.int32)[:, None] + in_blk
    valid = (rank < counts[blk_e][:, None]).reshape(p)
    src = jnp.where(valid, (start_sorted[blk_e].astype(jnp.int32)[:, None] + rank).reshape(p), 0)
    pair = s_pair[src]
    trash = jnp.broadcast_to(TOP_K * n + in_blk, (n_blocks, tm)).reshape(p)
    row_tok = jnp.where(valid, pair // TOP_K, 0).astype(jnp.int32)
    row_dst = jnp.where(valid, (pair % TOP_K) * n + pair // TOP_K, trash).astype(jnp.int32)
    row_gate = jnp.where(valid, s_gate[src], 0.0)
    return moe_ffn(h, norm2, (blk_e + layer * ne).astype(jnp.int32), n_used, row_tok, row_dst,
                   row_gate, w_gate, w_up, w_down, tm)


def odd_layer(h, b, l, layer, norm1, w_in, b_f, w_o, norm2, router, w_gate, w_up, w_down):
    d = h.shape[1]
    hd = FOX_HEADS * FOX_DIM
    tq = _query_tile(l - N_META)
    w_qkv = jnp.concatenate([w_in[:, :hd] * (FOX_DIM ** -0.5 * LOG2E), w_in[:, hd:3 * hd]],
                            axis=1).astype(BF16)
    w_f = jnp.zeros((d, LANES), F32).at[:, :FOX_HEADS].set(w_in[:, 3 * hd:]).astype(BF16)
    proj, gate_logit = norm_proj(h, 0, d, norm1, w_qkv, w_aux=w_f, name="fox_in_proj")
    b_f_p = jnp.zeros((1, LANES), F32).at[0, :FOX_HEADS].set(b_f.astype(F32))
    c = fox_decay(gate_logit, b_f_p, b, l)
    c = jnp.swapaxes(c[:, :, :FOX_HEADS], 1, 2)
    c_meta = jnp.zeros((b, FOX_HEADS, 1, BLOCK), F32).at[:, :, 0, :N_META].set(c[:, :, :N_META])
    c_real = c[:, :, N_META:].reshape(b, FOX_HEADS, (l - N_META) // tq, tq)
    o = fox_attention(proj, c_meta, c_real, b, l)
    h = out_proj([o], [w_o.astype(BF16)], h, name="fox_out_proj")
    return h, moe_layer(h, layer, norm2, router, w_gate, w_up, w_down)


def alibi_slopes(n):
    return jnp.asarray([2.0 ** (-8.0 * (i + 1) / n) for i in range(n)], dtype=F32)


def kernel(x, meta, ev_norm1, ev_w_in, ev_q_norm, ev_kv_norm, ev_w_uq, ev_w_ukv, ev_sinks, ev_w_o,
           ev_norm2, ev_w_gate_up, ev_w_down, od_norm1, od_w_in, od_b_f, od_w_o, od_norm2,
           od_router, od_w_gate, od_w_up, od_w_down, final_norm):
    b, seq, d = x.shape
    l = seq + N_META
    depth = ev_norm1.shape[0] + od_norm1.shape[0]
    assert depth % 2 == 0, "the trunk must end with an odd (MoE) layer"
    h = jnp.concatenate([jnp.broadcast_to(meta[None].astype(x.dtype), (b, N_META, d)), x], axis=1)
    h = h.reshape(b * l, d)
    cs = _rope_table(l)
    slopes = alibi_slopes(SWA_HEADS)
    ev_gu, ev_dn = ev_w_gate_up.astype(BF16), ev_w_down.astype(BF16)
    dff = od_w_gate.shape[-1]
    od_g = od_w_gate.astype(BF16).reshape(-1, d, dff)
    od_u = od_w_up.astype(BF16).reshape(-1, d, dff)
    od_d = od_w_down.astype(BF16).reshape(-1, dff, d)
    pending = None
    for layer in range(depth):
        i = layer // 2
        if layer % 2 == 0:
            h = even_layer(h, pending, b, l, i, ev_norm1[i], ev_w_in[i], ev_q_norm[i], ev_kv_norm[i],
                           ev_w_uq[i], ev_w_ukv[i], ev_sinks[i], ev_w_o[i], ev_norm2[i], ev_gu, ev_dn,
                           cs, slopes)
        else:
            h, pending = odd_layer(h, b, l, i, od_norm1[i], od_w_in[i], od_b_f[i], od_w_o[i],
                                   od_norm2[i], od_router[i], od_g, od_u, od_d)
    return final_rmsnorm(h, pending, final_norm, b, l)
```

```python
import functools

import jax
import jax.numpy as jnp
from jax import lax
from jax.experimental import pallas as pl
from jax.experimental.pallas import tpu as pltpu

F32 = jnp.float32
BF16 = jnp.bfloat16

N_META = 16
BLOCK = 128
EPS = 1e-6
MLA_HEADS = 8
Q_LORA = 512
KV_LORA = 256
QK_NOPE = 128
QK_ROPE = 64
V_DIM = 128
ROPE_THETA = 10000.0
SWA_HEADS = 8
SWA_KV_HEADS = 2
SWA_DIM = 128
SWA_GROUP = SWA_HEADS // SWA_KV_HEADS
FOX_HEADS = 16
FOX_DIM = 128
TOP_K = 2
MOE_ROWS = 512
BF16_ROWS = 16
WIDE_COLS = 2560
HEADS_PER_STEP = 4
LANES = 128
NEG = -1e30
LOG2E = 1.4426950408889634
V7X_VMEM_BYTES = 64 * 1024 * 1024
VMEM_RESERVE = 8 * 1024 * 1024
VMEM_LIMIT = V7X_VMEM_BYTES - VMEM_RESERVE


def _params(*sem):
    return pltpu.CompilerParams(dimension_semantics=sem, vmem_limit_bytes=VMEM_LIMIT)


def _pick(n, candidates):
    for c in candidates:
        if n % c == 0:
            return c
    return n


def _row_tile(n):
    return _pick(n, (688, 512, 384, 256, 128, 64, 32, 16))


def _norm_rows(x, g):
    ms = jnp.mean(x * x, axis=-1, keepdims=True)
    return (x * lax.rsqrt(ms + EPS)) * g


def _norm_proj_kernel(*refs, has_add, has_aux):
    refs = list(refs)
    x_ref = refs.pop(0)
    y0_ref, y1_ref = (refs.pop(0), refs.pop(0)) if has_add else (None, None)
    g_ref, w_ref = refs.pop(0), refs.pop(0)
    waux_ref = refs.pop(0) if has_aux else None
    o_ref = refs.pop(0)
    xsum_ref = refs.pop(0) if has_add else None
    oaux_ref = refs.pop(0) if has_aux else None
    xn_ref = refs.pop(0)

    @pl.when(pl.program_id(1) == 0)
    def _():
        x = x_ref[...].astype(F32)
        if has_add:
            x = x + y0_ref[...] + y1_ref[...]
            xsum_ref[...] = x
        xn = _norm_rows(x, g_ref[...]).astype(BF16)
        xn_ref[...] = xn
        if has_aux:
            oaux_ref[...] = jnp.dot(xn, waux_ref[...], preferred_element_type=F32)

    o_ref[...] = jnp.dot(xn_ref[...], w_ref[...], preferred_element_type=F32).astype(o_ref.dtype)


def norm_proj(x, xcol, k, gain, w, addends=None, w_aux=None, name="norm_proj"):
    n = x.shape[0]
    nout = w.shape[1]
    has_add = addends is not None
    has_aux = w_aux is not None
    tn = nout if nout <= WIDE_COLS else _pick(nout, (2048, 1024, 512, 256, 128))
    tm = _pick(n, (192, 128, 64, 32, 16)) if has_add else _row_tile(n)
    in_specs = [pl.BlockSpec((tm, k), lambda i, j: (i, xcol))]
    args = [x]
    if has_add:
        nblk = n // tm
        in_specs += [pl.BlockSpec((tm, k), lambda i, j: (i, 0)),
                     pl.BlockSpec((tm, k), lambda i, j: (nblk + i, 0))]
        args += [addends, addends]
    in_specs += [pl.BlockSpec((1, k), lambda i, j: (0, 0)), pl.BlockSpec((k, tn), lambda i, j: (0, j))]
    args += [gain.reshape(1, k).astype(F32), w]
    if has_aux:
        in_specs.append(pl.BlockSpec((k, w_aux.shape[1]), lambda i, j: (0, 0)))
        args.append(w_aux)
    out_shape = [jax.ShapeDtypeStruct((n, nout), BF16)]
    out_specs = [pl.BlockSpec((tm, tn), lambda i, j: (i, j))]
    if has_add:
        out_shape.append(jax.ShapeDtypeStruct((n, k), F32))
        out_specs.append(pl.BlockSpec((tm, k), lambda i, j: (i, 0)))
    if has_aux:
        out_shape.append(jax.ShapeDtypeStruct((n, w_aux.shape[1]), F32))
        out_specs.append(pl.BlockSpec((tm, w_aux.shape[1]), lambda i, j: (i, 0)))
    return pl.pallas_call(
        functools.partial(_norm_proj_kernel, has_add=has_add, has_aux=has_aux),
        grid=(n // tm, nout // tn),
        in_specs=in_specs,
        out_specs=out_specs,
        out_shape=out_shape,
        scratch_shapes=[pltpu.VMEM((tm, k), BF16)],
        compiler_params=_params("parallel", "arbitrary"),
        name=name,
    )(*args)


def _out_proj_kernel(*refs, n_lhs):
    a_refs = refs[:n_lhs]
    w_refs = refs[n_lhs:2 * n_lhs]
    res_ref, o_ref = refs[2 * n_lhs:]
    acc = res_ref[...]
    for a_ref, w_ref in zip(a_refs, w_refs):
        acc = acc + jnp.dot(a_ref[...], w_ref[...], preferred_element_type=F32)
    o_ref[...] = acc


def out_proj(lhs, ws, res, name="out_proj"):
    n, d = res.shape
    tm = _row_tile(n)
    tn = _pick(d, (2048, 1024, 512, 256, 128))
    in_specs = []
    for a in lhs:
        in_specs.append(pl.BlockSpec((tm, a.shape[1]), lambda i, j: (i, 0)))
    for w in ws:
        in_specs.append(pl.BlockSpec((w.shape[0], tn), lambda i, j: (0, j)))
    in_specs.append(pl.BlockSpec((tm, tn), lambda i, j: (i, j)))
    return pl.pallas_call(
        functools.partial(_out_proj_kernel, n_lhs=len(lhs)),
        grid=(n // tm, d // tn),
        in_specs=in_specs,
        out_specs=pl.BlockSpec((tm, tn), lambda i, j: (i, j)),
        out_shape=jax.ShapeDtypeStruct((n, d), F32),
        compiler_params=_params("parallel", "parallel"),
        name=name,
    )(*lhs, *ws, res)


def _router_kernel(x_ref, g_ref, wr_ref, logit_ref):
    xn = _norm_rows(x_ref[...], g_ref[...])
    logit_ref[...] = jnp.dot(xn, wr_ref[...], preferred_element_type=F32,
                             precision=lax.Precision.HIGHEST)


def router_logits(h, gain, router_padded):
    n, d = h.shape
    tm = _row_tile(n)
    ne = router_padded.shape[1]
    return pl.pallas_call(
        _router_kernel,
        grid=(n // tm,),
        in_specs=[pl.BlockSpec((tm, d), lambda i: (i, 0)),
                  pl.BlockSpec((1, d), lambda i: (0, 0)),
                  pl.BlockSpec((d, ne), lambda i: (0, 0))],
        out_specs=pl.BlockSpec((tm, ne), lambda i: (i, 0)),
        out_shape=jax.ShapeDtypeStruct((n, ne), F32),
        compiler_params=_params("parallel"),
        name="router_logits",
    )(h, gain.reshape(1, d).astype(F32), router_padded)


def _final_norm_kernel(x_ref, y0_ref, y1_ref, g_ref, o_ref):
    o_ref[...] = _norm_rows(x_ref[...] + y0_ref[...] + y1_ref[...], g_ref[...])


def final_rmsnorm(h, addends, gain, b, l):
    n, d = h.shape
    seq = l - N_META
    tr = _pick(seq, (512, 256, 128, 64, 32, 16))

    def rows(offset):
        def index(bi, s):
            return pl.multiple_of(offset + bi * l + N_META + s * tr, N_META), 0
        return pl.BlockSpec((pl.Element(tr), pl.Element(d)), index)

    return pl.pallas_call(
        _final_norm_kernel,
        grid=(b, seq // tr),
        in_specs=[rows(0), rows(0), rows(n), pl.BlockSpec((1, d), lambda bi, s: (0, 0))],
        out_specs=pl.BlockSpec((None, tr, d), lambda bi, s: (bi, s, 0)),
        out_shape=jax.ShapeDtypeStruct((b, seq, d), F32),
        compiler_params=_params("parallel", "parallel"),
        name="final_norm",
    )(h, addends, addends, gain.reshape(1, d).astype(F32))


def _swiglu_part(xn, wg, wu, wd):
    g = jnp.dot(xn, wg, preferred_element_type=F32)
    u = jnp.dot(xn, wu, preferred_element_type=F32)
    a = (g * jax.nn.sigmoid(g) * u).astype(BF16)
    return jnp.dot(a, wd, preferred_element_type=F32)


def _dense_ffn_kernel(x_ref, g_ref, wg_ref, wu_ref, wd_ref, o_ref, xn_ref):
    j = pl.program_id(1)

    @pl.when(j == 0)
    def _():
        x = x_ref[...]
        xn_ref[...] = _norm_rows(x, g_ref[...]).astype(BF16)
        o_ref[...] = x

    o_ref[...] += _swiglu_part(xn_ref[...], wg_ref[...], wu_ref[...], wd_ref[...])


def dense_ffn(h, gain, w_gate_up, w_down, layer):
    n, d = h.shape
    dff = w_down.shape[1]
    tm = _row_tile(n)
    tf = _pick(dff, (512, 256, 128))
    nj = dff // tf
    return pl.pallas_call(
        _dense_ffn_kernel,
        grid=(n // tm, nj),
        in_specs=[pl.BlockSpec((tm, d), lambda i, j: (i, 0)),
                  pl.BlockSpec((1, d), lambda i, j: (0, 0)),
                  pl.BlockSpec((None, d, tf), lambda i, j: (layer, 0, j)),
                  pl.BlockSpec((None, d, tf), lambda i, j: (layer, 0, nj + j)),
                  pl.BlockSpec((None, tf, d), lambda i, j: (layer, j, 0))],
        out_specs=pl.BlockSpec((tm, d), lambda i, j: (i, 0)),
        out_shape=jax.ShapeDtypeStruct((n, d), F32),
        scratch_shapes=[pltpu.VMEM((tm, d), BF16)],
        compiler_params=_params("parallel", "arbitrary"),
        name="dense_ffn",
    )(h, gain.reshape(1, d).astype(F32), w_gate_up, w_gate_up, w_down)


def _moe_ffn_kernel(be_ref, nused_ref, tok_ref, dst_ref,
                    h_hbm, gain_ref, gate_ref, wg_ref, wu_ref, wd_ref, y_hbm,
                    xbuf, xn_ref, acc_ref, ybuf, gsem, ssem, *, tm, nj, rows_per_step):
    b = pl.program_id(0)
    j = pl.program_id(1)
    n_used = nused_ref[0]

    def gather_rows(blk, lo, hi):
        slot = blk % 2

        def body(r, c):
            tok = tok_ref[blk * tm + r]
            pltpu.make_async_copy(h_hbm.at[pl.ds(tok, 1)], xbuf.at[slot, pl.ds(r, 1)],
                                  gsem.at[slot]).start()
            return c

        lax.fori_loop(lo, hi, body, 0)

    def scatter_rows(blk, lo, hi):
        def body(r, c):
            dst = dst_ref[blk * tm + r]
            pltpu.make_async_copy(ybuf.at[pl.ds(r, 1)], y_hbm.at[pl.ds(dst, 1)], ssem.at[0]).start()
            return c

        lax.fori_loop(lo, hi, body, 0)

    def wait_scatter():
        pltpu.make_async_copy(ybuf, y_hbm.at[pl.ds(0, tm)], ssem.at[0]).wait()

    def wait_gather(slot):
        pltpu.make_async_copy(h_hbm.at[pl.ds(0, tm)], xbuf.at[slot], gsem.at[slot]).wait()

    trash0 = y_hbm.shape[0] - tm
    nxt = jnp.minimum(b + 1, n_used - 1)
    prv = jnp.maximum(b - 1, 0)

    def move_rows(rows):
        for r in rows:
            pltpu.make_async_copy(h_hbm.at[pl.ds(tok_ref[nxt * tm + r], 1)],
                                  xbuf.at[(b + 1) % 2, pl.ds(r, 1)], gsem.at[(b + 1) % 2]).start()
            dst = jnp.where(b > 0, dst_ref[prv * tm + r], trash0 + r)
            pltpu.make_async_copy(ybuf.at[pl.ds(r, 1)], y_hbm.at[pl.ds(dst, 1)], ssem.at[0]).start()

    @pl.when(b < n_used)
    def _():
        @pl.when(jnp.logical_and(b == 0, j == 0))
        def _():
            gather_rows(0, 0, tm)
            ybuf[...] = jnp.zeros_like(ybuf)
            init = pltpu.make_async_copy(ybuf, y_hbm.at[pl.ds(trash0, tm)], ssem.at[0])
            init.start()
            init.wait()

        @pl.when(j == 0)
        def _():
            wait_gather(b % 2)
            xn_ref[...] = _norm_rows(xbuf[b % 2], gain_ref[...]).astype(BF16)
            acc_ref[...] = jnp.zeros_like(acc_ref)
            move_rows([nj * rows_per_step + r for r in range(tm - nj * rows_per_step)])

        move_rows([j * rows_per_step + k for k in range(rows_per_step)])
        acc_ref[...] += _swiglu_part(xn_ref[...], wg_ref[0], wu_ref[0], wd_ref[0])

        @pl.when(j == nj - 1)
        def _():
            wait_scatter()
            ybuf[...] = acc_ref[...] * gate_ref[...]

            @pl.when(b == n_used - 1)
            def _():
                wait_gather((b + 1) % 2)
                scatter_rows(b, 0, tm)
                wait_scatter()


def moe_ffn(h, gain, blk_e, n_used, row_tok, row_dst, row_gate, w_gate, w_up, w_down, tm):
    n, d = h.shape
    dff = w_down.shape[1]
    tf = _pick(dff, (1024, 512, 256, 128))
    nj = dff // tf
    nb = row_tok.shape[0] // tm
    rows_per_step = tm // nj

    def jj(b, j, nused):
        return jnp.where(b < nused[0], j, nj - 1)

    grid_spec = pltpu.PrefetchScalarGridSpec(
        num_scalar_prefetch=4,
        grid=(nb, nj),
        in_specs=[pl.BlockSpec(memory_space=pl.ANY),
                  pl.BlockSpec((1, d), lambda b, j, be, nu, *_: (0, 0)),
                  pl.BlockSpec((tm, 1), lambda b, j, be, nu, *_: (b, 0)),
                  pl.BlockSpec((1, d, tf), lambda b, j, be, nu, *_: (be[b], 0, jj(b, j, nu))),
                  pl.BlockSpec((1, d, tf), lambda b, j, be, nu, *_: (be[b], 0, jj(b, j, nu))),
                  pl.BlockSpec((1, tf, d), lambda b, j, be, nu, *_: (be[b], jj(b, j, nu), 0))],
        out_specs=pl.BlockSpec(memory_space=pl.ANY),
        scratch_shapes=[pltpu.VMEM((2, tm, d), F32), pltpu.VMEM((tm, d), BF16),
                        pltpu.VMEM((tm, d), F32), pltpu.VMEM((tm, d), F32),
                        pltpu.SemaphoreType.DMA((2,)), pltpu.SemaphoreType.DMA((1,))],
    )
    return pl.pallas_call(
        functools.partial(_moe_ffn_kernel, tm=tm, nj=nj, rows_per_step=rows_per_step),
        grid_spec=grid_spec,
        out_shape=jax.ShapeDtypeStruct((TOP_K * n + tm, d), F32),
        compiler_params=_params("arbitrary", "arbitrary"),
        name="moe_ffn",
    )(blk_e, n_used, row_tok, row_dst, h, gain.reshape(1, d).astype(F32),
      row_gate.reshape(-1, 1), w_gate, w_up, w_down)


def _qk(q, k):
    return lax.dot_general(q, k, (((1,), (1,)), ((), ())), preferred_element_type=F32)


def _pv(p, v):
    return jnp.dot(p.astype(BF16), v, preferred_element_type=F32)


def _flash_update(s, v, m, l, acc):
    m_new = jnp.maximum(m, jnp.max(s, axis=-1, keepdims=True))
    alpha = jnp.exp2(m - m_new)
    p = jnp.exp2(s - m_new)
    l = alpha * l + jnp.sum(p, axis=-1, keepdims=True)
    acc = alpha * acc + _pv(p, v)
    return m_new, l, acc


def _iota2(shape):
    return (lax.broadcasted_iota(jnp.int32, shape, 0), lax.broadcasted_iota(jnp.int32, shape, 1))


def _query_tile(seq):
    return _pick(seq, (512, 256, 128))


def _causal_sweep(heads, nq, tq):
    r16, c16 = _iota2((N_META, BLOCK))
    meta_kv = [(k_ref[0:BLOCK, :], v_ref[0:BLOCK, :]) for _, k_ref, v_ref, _, _ in heads]

    for (q_ref, _, _, o_ref, col_bias), (k0, v0) in zip(heads, meta_kv):
        s = _qk(q_ref[0:N_META, :], k0) + col_bias(None)
        s = jnp.where(c16 <= r16, s, NEG)
        p = jnp.exp2(s - jnp.max(s, axis=-1, keepdims=True))
        o = _pv(p, v0) * (1.0 / jnp.sum(p, axis=-1, keepdims=True))
        o_ref[0:N_META, :] = o.astype(o_ref.dtype)

    def qblock(i, carry):
        qs = pl.multiple_of(N_META + i * tq, N_META)
        cm = lax.broadcasted_iota(jnp.int32, (tq, BLOCK), 1)
        r, c = _iota2((tq, tq))
        qs_ = []
        state = []
        for (q_ref, k_ref, v_ref, _, col_bias), (k0, v0) in zip(heads, meta_kv):
            q = q_ref[pl.ds(qs, tq), :]
            s0 = jnp.where(cm < N_META, _qk(q, k0) + col_bias(None), NEG)
            s1 = jnp.where(c <= r, _qk(q, k_ref[pl.ds(qs, tq), :]) + col_bias(i), NEG)
            m = jnp.maximum(jnp.max(s0, axis=-1, keepdims=True), jnp.max(s1, axis=-1, keepdims=True))
            p0 = jnp.exp2(s0 - m)
            p1 = jnp.exp2(s1 - m)
            l = jnp.sum(p0, axis=-1, keepdims=True) + jnp.sum(p1, axis=-1, keepdims=True)
            qs_.append(q)
            state.append((m, l, _pv(p0, v0) + _pv(p1, v_ref[pl.ds(qs, tq), :])))

        def chunk(t, state):
            ks = pl.multiple_of(N_META + t * tq, N_META)
            new = []
            for (_, k_ref, v_ref, _, col_bias), q, mla in zip(heads, qs_, state):
                s = _qk(q, k_ref[pl.ds(ks, tq), :]) + col_bias(t)
                new.append(_flash_update(s, v_ref[pl.ds(ks, tq), :], *mla))
            return tuple(new)

        state = lax.fori_loop(0, i, chunk, tuple(state))
        for (_, _, _, o_ref, _), (m, l, acc) in zip(heads, state):
            o_ref[pl.ds(qs, tq), :] = (acc * (1.0 / l)).astype(o_ref.dtype)
        return carry

    lax.fori_loop(0, nq, qblock, 0)


def _fox_kernel(q_ref, k_ref, v_ref, cm_ref, cr_ref, o_ref, *, nq, tq):
    def head(g):
        cols = slice(g * FOX_DIM, (g + 1) * FOX_DIM)

        def col_bias(t):
            return -cm_ref[g] if t is None else -cr_ref[g, pl.ds(t, 1), :]

        return (q_ref.at[:, cols], k_ref.at[:, cols], v_ref.at[:, cols], o_ref.at[:, cols], col_bias)

    _causal_sweep([head(g) for g in range(HEADS_PER_STEP)], nq, tq)


def fox_attention(proj, c_meta, c_real, b, l):
    nq, tq = c_real.shape[2:]
    h = FOX_HEADS // HEADS_PER_STEP
    gw = HEADS_PER_STEP * FOX_DIM
    p3 = proj.reshape(b, l, 3 * FOX_HEADS * FOX_DIM)
    blk = lambda off: pl.BlockSpec((None, l, gw), lambda bi, hi: (bi, 0, off + hi))
    out = pl.pallas_call(
        functools.partial(_fox_kernel, nq=nq, tq=tq),
        grid=(b, h),
        in_specs=[blk(0), blk(h), blk(2 * h),
                  pl.BlockSpec((None, HEADS_PER_STEP, 1, BLOCK), lambda bi, hi: (bi, hi, 0, 0)),
                  pl.BlockSpec((None, HEADS_PER_STEP, nq, tq), lambda bi, hi: (bi, hi, 0, 0))],
        out_specs=pl.BlockSpec((None, l, gw), lambda bi, hi: (bi, 0, hi)),
        out_shape=jax.ShapeDtypeStruct((b, l, FOX_HEADS * FOX_DIM), BF16),
        compiler_params=_params("parallel", "parallel"),
        name="fox_attention",
    )(p3, p3, p3, c_meta, c_real)
    return out.reshape(b * l, FOX_HEADS * FOX_DIM)


def _fox_decay_kernel(g_ref, bf_ref, c_ref, *, nchunk):
    r, c = _iota2((BLOCK, BLOCK))
    tri = jnp.where(c <= r, 1.0, 0.0).astype(F32)
    bf = bf_ref[...]

    def log_sigmoid(x):
        return jnp.minimum(x, 0.0) - jnp.log1p(jnp.exp(-jnp.abs(x)))

    def chunk_cumsum(start):
        lf = log_sigmoid(g_ref[pl.ds(start, BLOCK), :] + bf)
        return jnp.dot(tri, lf, preferred_element_type=F32, precision=lax.Precision.HIGHEST)

    head = chunk_cumsum(0)[0:N_META, :]
    c_ref[0:N_META, :] = head * LOG2E

    def body(t, carry):
        start = pl.multiple_of(N_META + t * BLOCK, N_META)
        cc = chunk_cumsum(start) + carry
        c_ref[pl.ds(start, BLOCK), :] = cc * LOG2E
        return cc[BLOCK - 1:BLOCK, :]

    lax.fori_loop(0, nchunk, body, head[N_META - 1:N_META, :])


def fox_decay(gate_logit, b_f_padded, b, l):
    g3 = gate_logit.reshape(b, l, LANES)
    return pl.pallas_call(
        functools.partial(_fox_decay_kernel, nchunk=(l - N_META) // BLOCK),
        grid=(b,),
        in_specs=[pl.BlockSpec((None, l, LANES), lambda bi: (bi, 0, 0)),
                  pl.BlockSpec((1, LANES), lambda bi: (0, 0))],
        out_specs=pl.BlockSpec((None, l, LANES), lambda bi: (bi, 0, 0)),
        out_shape=jax.ShapeDtypeStruct((b, l, LANES), F32),
        compiler_params=_params("parallel"),
        name="fox_decay",
    )(g3, b_f_padded)


def _mla_kernel(q_ref, kv_ref, kr_ref, cs_ref, o_ref, qx_ref, kx_ref, *, nq, tq):
    cs = cs_ref[...]
    lane = lax.broadcasted_iota(jnp.int32, cs.shape, 1)
    ek = kr_ref[...].astype(F32) * cs
    rk = jnp.where(lane < QK_ROPE, ek + pltpu.roll(ek, QK_ROPE, axis=1), 0.0).astype(BF16)
    w = QK_NOPE + 2 * QK_ROPE
    heads = []
    for g in range(HEADS_PER_STEP):
        eq = q_ref[:, g * w + QK_NOPE:(g + 1) * w].astype(F32) * cs
        qx_ref[g, :, :QK_NOPE] = q_ref[:, g * w:g * w + QK_NOPE]
        qx_ref[g, :, QK_NOPE:] = (eq + pltpu.roll(eq, QK_ROPE, axis=1)).astype(BF16)
        kx_ref[g, :, :QK_NOPE] = kv_ref[:, g * w:g * w + QK_NOPE]
        kx_ref[g, :, QK_NOPE:] = rk
        heads.append((qx_ref.at[g], kx_ref.at[g], kv_ref.at[:, g * w + QK_NOPE:(g + 1) * w],
                      o_ref.at[:, g * V_DIM:(g + 1) * V_DIM], lambda t: 0.0))
    _causal_sweep(heads, nq, tq)


def mla_attention(qa, kv, proj, kr_col, cs, b, l):
    tq = _query_tile(l - N_META)
    nq = (l - N_META) // tq
    h = MLA_HEADS // HEADS_PER_STEP
    w = QK_NOPE + 2 * QK_ROPE
    gw = HEADS_PER_STEP * w
    qa3 = qa.reshape(b, l, MLA_HEADS * w)
    kv3 = kv.reshape(b, l, MLA_HEADS * w)
    p3 = proj.reshape(b, l, proj.shape[1])
    out = pl.pallas_call(
        functools.partial(_mla_kernel, nq=nq, tq=tq),
        grid=(b, h),
        in_specs=[pl.BlockSpec((None, l, gw), lambda bi, hi: (bi, 0, hi)),
                  pl.BlockSpec((None, l, gw), lambda bi, hi: (bi, 0, hi)),
                  pl.BlockSpec((None, l, 2 * QK_ROPE), lambda bi, hi: (bi, 0, kr_col)),
                  pl.BlockSpec((l, 2 * QK_ROPE), lambda bi, hi: (0, 0))],
        out_specs=pl.BlockSpec((None, l, HEADS_PER_STEP * V_DIM), lambda bi, hi: (bi, 0, hi)),
        out_shape=jax.ShapeDtypeStruct((b, l, MLA_HEADS * V_DIM), BF16),
        scratch_shapes=[pltpu.VMEM((HEADS_PER_STEP, l, w), BF16),
                        pltpu.VMEM((HEADS_PER_STEP, l, w), BF16)],
        compiler_params=_params("parallel", "parallel"),
        name="mla_attention",
    )(qa3, kv3, p3, cs)
    return out.reshape(b * l, MLA_HEADS * V_DIM)


def _swa_kernel(sink_ref, slope_ref, q_ref, k_ref, v_ref, o_ref, *, nb):
    kvh = pl.program_id(1)
    r16, c16 = _iota2((N_META, BLOCK))
    rel16 = (r16 - c16).astype(F32)
    r3, c3 = _iota2((BLOCK, 3 * BLOCK))
    in_meta = c3 < BLOCK
    in_prev = jnp.logical_and(c3 >= BLOCK, c3 < 2 * BLOCK)
    rel3 = (r3 - c3).astype(F32)
    allowed = jnp.logical_or(
        jnp.logical_or(c3 < N_META, jnp.logical_and(in_prev, c3 - BLOCK > r3)),
        jnp.logical_and(c3 >= 2 * BLOCK, c3 - 2 * BLOCK <= r3))
    mask_bias = jnp.where(allowed, 0.0, NEG)
    prev_bias = jnp.where(in_prev, NEG, 0.0)
    k0 = k_ref[0:BLOCK, :]
    v0 = v_ref[0:BLOCK, :]
    sinks = [sink_ref[kvh * SWA_GROUP + g] for g in range(SWA_GROUP)]
    slopes = [slope_ref[kvh * SWA_GROUP + g] for g in range(SWA_GROUP)]

    for g in range(SWA_GROUP):
        cols = slice(g * SWA_DIM, (g + 1) * SWA_DIM)
        s = _qk(q_ref[0:N_META, cols], k0) - slopes[g] * rel16
        s = jnp.where(c16 <= r16, s, NEG)
        m = jnp.maximum(jnp.max(s, axis=-1, keepdims=True), sinks[g])
        p = jnp.exp2(s - m)
        l = jnp.sum(p, axis=-1, keepdims=True) + jnp.exp2(sinks[g] - m)
        o_ref[0:N_META, cols] = (_pv(p, v0) * (1.0 / l)).astype(o_ref.dtype)

    def qblock(i, carry):
        qs = pl.multiple_of(N_META + i * BLOCK, N_META)
        ps = pl.multiple_of(jnp.where(i > 0, qs - BLOCK, qs), N_META)
        kcat = jnp.concatenate([k0, k_ref[pl.ds(ps, BLOCK), :], k_ref[pl.ds(qs, BLOCK), :]], axis=0)
        vcat = jnp.concatenate([v0, v_ref[pl.ds(ps, BLOCK), :], v_ref[pl.ds(qs, BLOCK), :]], axis=0)
        dist = rel3 + jnp.where(in_meta, jnp.asarray(N_META + i * BLOCK, F32), float(2 * BLOCK))
        bias = mask_bias + prev_bias * jnp.where(i > 0, 0.0, 1.0)
        for g in range(SWA_GROUP):
            cols = slice(g * SWA_DIM, (g + 1) * SWA_DIM)
            s = _qk(q_ref[pl.ds(qs, BLOCK), cols], kcat) + (bias - slopes[g] * dist)
            m = jnp.maximum(jnp.max(s, axis=-1, keepdims=True), sinks[g])
            p = jnp.exp2(s - m)
            l = jnp.sum(p, axis=-1, keepdims=True) + jnp.exp2(sinks[g] - m)
            o_ref[pl.ds(qs, BLOCK), cols] = (_pv(p, vcat) * (1.0 / l)).astype(o_ref.dtype)
        return carry

    lax.fori_loop(0, nb, qblock, 0, unroll=2 if nb % 2 == 0 else 1)


def swa_attention(proj, q_col, k_col, v_col, sinks, slopes, b, l):
    nb = (l - N_META) // BLOCK
    gw = SWA_GROUP * SWA_DIM
    p3 = proj.reshape(b, l, proj.shape[1])
    grid_spec = pltpu.PrefetchScalarGridSpec(
        num_scalar_prefetch=2,
        grid=(b, SWA_KV_HEADS),
        in_specs=[pl.BlockSpec((None, l, gw), lambda bi, hi, s0, s1: (bi, 0, q_col + hi)),
                  pl.BlockSpec((None, l, SWA_DIM), lambda bi, hi, s0, s1: (bi, 0, k_col + hi)),
                  pl.BlockSpec((None, l, SWA_DIM), lambda bi, hi, s0, s1: (bi, 0, v_col + hi))],
        out_specs=pl.BlockSpec((None, l, gw), lambda bi, hi, s0, s1: (bi, 0, hi)),
    )
    out = pl.pallas_call(
        functools.partial(_swa_kernel, nb=nb),
        grid_spec=grid_spec,
        out_shape=jax.ShapeDtypeStruct((b, l, SWA_HEADS * SWA_DIM), BF16),
        compiler_params=_params("parallel", "parallel"),
        name="swa_attention",
    )(sinks, slopes, p3, p3, p3)
    return out.reshape(b * l, SWA_HEADS * SWA_DIM)


def _rope_swap_cols(w):
    half = QK_ROPE // 2
    return jnp.concatenate([-w[..., half:], w[..., :half]], axis=-1)


def _rope_table(l):
    half = QK_ROPE // 2
    inv = ROPE_THETA ** (-jnp.arange(half, dtype=F32) / half)
    ang = jnp.arange(l, dtype=F32)[:, None] * inv[None, :]
    cos, sin = jnp.cos(ang), jnp.sin(ang)
    return jnp.concatenate([cos, cos, sin, sin], axis=-1)


def even_layer(h, pending, b, l, layer, norm1, w_in, q_norm, kv_norm, w_uq, w_ukv, sinks, w_o, norm2,
               w_gate_up, w_down, cs, slopes):
    d = h.shape[1]
    o1, o2, o3 = Q_LORA, Q_LORA + KV_LORA, Q_LORA + KV_LORA + QK_ROPE
    o4 = o3 + SWA_HEADS * SWA_DIM
    o5 = o4 + SWA_KV_HEADS * SWA_DIM
    w_kr = w_in[:, o2:o3]
    w_in2 = jnp.concatenate([w_in[:, o3:o4] * (SWA_DIM ** -0.5 * LOG2E), w_in[:, :o1], w_in[:, o1:o2],
                             w_in[:, o4:o5], w_in[:, o5:], w_kr, _rope_swap_cols(w_kr)],
                            axis=1).astype(BF16)
    w_in2 = jnp.pad(w_in2, ((0, 0), (0, -w_in2.shape[1] % 512)))
    res = norm_proj(h, 0, d, norm1, w_in2, addends=pending, name="even_in_proj")
    proj = res[0]
    if pending is not None:
        h = res[1]
    cq_col = (SWA_HEADS * SWA_DIM) // Q_LORA
    ckv_col = (SWA_HEADS * SWA_DIM + Q_LORA) // KV_LORA
    ks_col = (SWA_HEADS * SWA_DIM + Q_LORA + KV_LORA) // SWA_DIM
    vs_col = ks_col + SWA_KV_HEADS
    kr_col = vs_col + SWA_KV_HEADS

    wq = w_uq.reshape(Q_LORA, MLA_HEADS, QK_NOPE + QK_ROPE) * ((QK_NOPE + QK_ROPE) ** -0.5 * LOG2E)
    wq2 = jnp.concatenate([wq, _rope_swap_cols(wq[..., QK_NOPE:])], axis=-1)
    wq2 = wq2.reshape(Q_LORA, MLA_HEADS * (QK_NOPE + 2 * QK_ROPE)).astype(BF16)
    qa = norm_proj(proj, cq_col, Q_LORA, q_norm, wq2, name="mla_q_up")[0]
    kv = norm_proj(proj, ckv_col, KV_LORA, kv_norm, w_ukv.astype(BF16), name="mla_kv_up")[0]

    o_a = mla_attention(qa, kv, proj, kr_col, cs, b, l)
    o_b = swa_attention(proj, 0, ks_col, vs_col, sinks.astype(F32) * LOG2E, slopes * LOG2E, b, l)
    na = MLA_HEADS * V_DIM
    h = out_proj([o_a, o_b], [w_o[:na].astype(BF16), w_o[na:].astype(BF16)], h, name="even_out_proj")
    return dense_ffn(h, norm2, w_gate_up, w_down, layer)


def moe_layer(h, layer, norm2, router, w_gate, w_up, w_down):
    n, d = h.shape
    ne = router.shape[1]
    router_p = jnp.zeros((d, LANES), F32).at[:, :ne].set(router)
    logits = router_logits(h, norm2, router_p)
    top_val, top_idx = lax.top_k(logits[:, :ne], TOP_K)
    gates = jax.nn.softmax(top_val, axis=-1)
    a = n * TOP_K
    tm = min(MOE_ROWS, -(-(a // (8 * ne)) // BF16_ROWS) * BF16_ROWS)
    flat_e = top_idx.reshape(-1).astype(jnp.int32)
    _, s_pair, s_gate = lax.sort((flat_e, jnp.arange(a, dtype=jnp.int32), gates.reshape(-1)),
                                 num_keys=1, is_stable=True)
    counts = jnp.sum(flat_e[:, None] == jnp.arange(ne, dtype=jnp.int32)[None, :], axis=0)
    padded = (counts + tm - 1) // tm * tm
    start_sorted = jnp.cumsum(counts) - counts
    end_pad = jnp.cumsum(padded)
    start_pad = end_pad - padded
    n_blocks = -(-a // tm) + ne
    p = n_blocks * tm
    blk_start = jnp.arange(n_blocks) * tm
    blk_e = jnp.minimum(jnp.sum(blk_start[:, None] >= end_pad[None, :], axis=1), ne - 1)
    n_used = (end_pad[-1] // tm).astype(jnp.int32).reshape(1)
    in_blk = jnp.arange(tm, dtype=jnp.int32)[None, :]
    rank = (blk_start - start_pad[blk_e]).astype(jnp.int32)[:, None] + in_blk
    valid = (rank < counts[blk_e][:, None]).reshape(p)
    src = jnp.where(valid, (start_sorted[blk_e].astype(jnp.int32)[:, None] + rank).reshape(p), 0)
    pair = s_pair[src]
    trash = jnp.broadcast_to(TOP_K * n + in_blk, (n_blocks, tm)).reshape(p)
    row_tok = jnp.where(valid, pair // TOP_K, 0).astype(jnp.int32)
    row_dst = jnp.where(valid, (pair % TOP_K) * n + pair // TOP_K, trash).astype(jnp.int32)
    row_gate = jnp.where(valid, s_gate[src], 0.0)
    return moe_ffn(h, norm2, (blk_e + layer * ne).astype(jnp.int32), n_used, row_tok, row_dst,
                   row_gate, w_gate, w_up, w_down, tm)


def odd_layer(h, b, l, layer, norm1, w_in, b_f, w_o, norm2, router, w_gate, w_up, w_down):
    d = h.shape[1]
    hd = FOX_HEADS * FOX_DIM
    tq = _query_tile(l - N_META)
    w_qkv = jnp.concatenate([w_in[:, :hd] * (FOX_DIM ** -0.5 * LOG2E), w_in[:, hd:3 * hd]],
                            axis=1).astype(BF16)
    w_f = jnp.zeros((d, LANES), F32).at[:, :FOX_HEADS].set(w_in[:, 3 * hd:]).astype(BF16)
    proj, gate_logit = norm_proj(h, 0, d, norm1, w_qkv, w_aux=w_f, name="fox_in_proj")
    b_f_p = jnp.zeros((1, LANES), F32).at[0, :FOX_HEADS].set(b_f.astype(F32))
    c = fox_decay(gate_logit, b_f_p, b, l)
    c = jnp.swapaxes(c[:, :, :FOX_HEADS], 1, 2)
    c_meta = jnp.zeros((b, FOX_HEADS, 1, BLOCK), F32).at[:, :, 0, :N_META].set(c[:, :, :N_META])
    c_real = c[:, :, N_META:].reshape(b, FOX_HEADS, (l - N_META) // tq, tq)
    o = fox_attention(proj, c_meta, c_real, b, l)
    h = out_proj([o], [w_o.astype(BF16)], h, name="fox_out_proj")
    return h, moe_layer(h, layer, norm2, router, w_gate, w_up, w_down)


def alibi_slopes(n):
    return jnp.asarray([2.0 ** (-8.0 * (i + 1) / n) for i in range(n)], dtype=F32)


def kernel(x, meta, ev_norm1, ev_w_in, ev_q_norm, ev_kv_norm, ev_w_uq, ev_w_ukv, ev_sinks, ev_w_o,
           ev_norm2, ev_w_gate_up, ev_w_down, od_norm1, od_w_in, od_b_f, od_w_o, od_norm2,
           od_router, od_w_gate, od_w_up, od_w_down, final_norm):
    b, seq, d = x.shape
    l = seq + N_META
    depth = ev_norm1.shape[0] + od_norm1.shape[0]
    assert depth % 2 == 0, "the trunk must end with an odd (MoE) layer"
    h = jnp.concatenate([jnp.broadcast_to(meta[None].astype(x.dtype), (b, N_META, d)), x], axis=1)
    h = h.reshape(b * l, d)
    cs = _rope_table(l)
    slopes = alibi_slopes(SWA_HEADS)
    ev_gu, ev_dn = ev_w_gate_up.astype(BF16), ev_w_down.astype(BF16)
    dff = od_w_gate.shape[-1]
    od_g = od_w_gate.astype(BF16).reshape(-1, d, dff)
    od_u = od_w_up.astype(BF16).reshape(-1, d, dff)
    od_d = od_w_down.astype(BF16).reshape(-1, dff, d)
    pending = None
    for layer in range(depth):
        i = layer // 2
        if layer % 2 == 0:
            h = even_layer(h, pending, b, l, i, ev_norm1[i], ev_w_in[i], ev_q_norm[i], ev_kv_norm[i],
                           ev_w_uq[i], ev_w_ukv[i], ev_sinks[i], ev_w_o[i], ev_norm2[i], ev_gu, ev_dn,
                           cs, slopes)
        else:
            h, pending = odd_layer(h, b, l, i, od_norm1[i], od_w_in[i], od_b_f[i], od_w_o[i],
                                   od_norm2[i], od_router[i], od_g, od_u, od_d)
    return final_rmsnorm(h, pending, final_norm, b, l)
```

```python
import functools

import jax
import jax.numpy as jnp
from jax import lax
from jax.experimental import pallas as pl
from jax.experimental.pallas import tpu as pltpu

F32 = jnp.float32
BF16 = jnp.bfloat16

N_META = 16
BLOCK = 128
EPS = 1e-6
MLA_HEADS = 8
Q_LORA = 512
KV_LORA = 256
QK_NOPE = 128
QK_ROPE = 64
V_DIM = 128
ROPE_THETA = 10000.0
SWA_HEADS = 8
SWA_KV_HEADS = 2
SWA_DIM = 128
SWA_GROUP = SWA_HEADS // SWA_KV_HEADS
FOX_HEADS = 16
FOX_DIM = 128
TOP_K = 2
MOE_ROWS = 512
BF16_ROWS = 16
WIDE_COLS = 2560
HEADS_PER_STEP = 4
LANES = 128
NEG = -1e30
LOG2E = 1.4426950408889634
V7X_VMEM_BYTES = 64 * 1024 * 1024
VMEM_RESERVE = 8 * 1024 * 1024
VMEM_LIMIT = V7X_VMEM_BYTES - VMEM_RESERVE


def _params(*sem):
    return pltpu.CompilerParams(dimension_semantics=sem, vmem_limit_bytes=VMEM_LIMIT)


def _pick(n, candidates):
    for c in candidates:
        if n % c == 0:
            return c
    return n


def _row_tile(n):
    return _pick(n, (688, 512, 384, 256, 128, 64, 32, 16))


def _norm_rows(x, g):
    ms = jnp.mean(x * x, axis=-1, keepdims=True)
    return (x * lax.rsqrt(ms + EPS)) * g


def _norm_proj_kernel(*refs, has_add, has_aux):
    refs = list(refs)
    x_ref = refs.pop(0)
    y0_ref, y1_ref = (refs.pop(0), refs.pop(0)) if has_add else (None, None)
    g_ref, w_ref = refs.pop(0), refs.pop(0)
    waux_ref = refs.pop(0) if has_aux else None
    o_ref = refs.pop(0)
    xsum_ref = refs.pop(0) if has_add else None
    oaux_ref = refs.pop(0) if has_aux else None
    xn_ref = refs.pop(0)

    @pl.when(pl.program_id(1) == 0)
    def _():
        x = x_ref[...].astype(F32)
        if has_add:
            x = x + y0_ref[...] + y1_ref[...]
            xsum_ref[...] = x
        xn = _norm_rows(x, g_ref[...]).astype(BF16)
        xn_ref[...] = xn
        if has_aux:
            oaux_ref[...] = jnp.dot(xn, waux_ref[...], preferred_element_type=F32)

    o_ref[...] = jnp.dot(xn_ref[...], w_ref[...], preferred_element_type=F32).astype(o_ref.dtype)


def norm_proj(x, xcol, k, gain, w, addends=None, w_aux=None, name="norm_proj"):
    n = x.shape[0]
    nout = w.shape[1]
    has_add = addends is not None
    has_aux = w_aux is not None
    tn = nout if nout <= WIDE_COLS else _pick(nout, (2048, 1024, 512, 256, 128))
    tm = _pick(n, (192, 128, 64, 32, 16)) if has_add else _row_tile(n)
    in_specs = [pl.BlockSpec((tm, k), lambda i, j: (i, xcol))]
    args = [x]
    if has_add:
        nblk = n // tm
        in_specs += [pl.BlockSpec((tm, k), lambda i, j: (i, 0)),
                     pl.BlockSpec((tm, k), lambda i, j: (nblk + i, 0))]
        args += [addends, addends]
    in_specs += [pl.BlockSpec((1, k), lambda i, j: (0, 0)), pl.BlockSpec((k, tn), lambda i, j: (0, j))]
    args += [gain.reshape(1, k).astype(F32), w]
    if has_aux:
        in_specs.append(pl.BlockSpec((k, w_aux.shape[1]), lambda i, j: (0, 0)))
        args.append(w_aux)
    out_shape = [jax.ShapeDtypeStruct((n, nout), BF16)]
    out_specs = [pl.BlockSpec((tm, tn), lambda i, j: (i, j))]
    if has_add:
        out_shape.append(jax.ShapeDtypeStruct((n, k), F32))
        out_specs.append(pl.BlockSpec((tm, k), lambda i, j: (i, 0)))
    if has_aux:
        out_shape.append(jax.ShapeDtypeStruct((n, w_aux.shape[1]), F32))
        out_specs.append(pl.BlockSpec((tm, w_aux.shape[1]), lambda i, j: (i, 0)))
    return pl.pallas_call(
        functools.partial(_norm_proj_kernel, has_add=has_add, has_aux=has_aux),
        grid=(n // tm, nout // tn),
        in_specs=in_specs,
        out_specs=out_specs,
        out_shape=out_shape,
        scratch_shapes=[pltpu.VMEM((tm, k), BF16)],
        compiler_params=_params("parallel", "arbitrary"),
        name=name,
    )(*args)


def _out_proj_kernel(*refs, n_lhs):
    a_refs = refs[:n_lhs]
    w_refs = refs[n_lhs:2 * n_lhs]
    res_ref, o_ref = refs[2 * n_lhs:]
    acc = res_ref[...]
    for a_ref, w_ref in zip(a_refs, w_refs):
        acc = acc + jnp.dot(a_ref[...], w_ref[...], preferred_element_type=F32)
    o_ref[...] = acc


def out_proj(lhs, ws, res, name="out_proj"):
    n, d = res.shape
    tm = _row_tile(n)
    tn = _pick(d, (2048, 1024, 512, 256, 128))
    in_specs = []
    for a in lhs:
        in_specs.append(pl.BlockSpec((tm, a.shape[1]), lambda i, j: (i, 0)))
    for w in ws:
        in_specs.append(pl.BlockSpec((w.shape[0], tn), lambda i, j: (0, j)))
    in_specs.append(pl.BlockSpec((tm, tn), lambda i, j: (i, j)))
    return pl.pallas_call(
        functools.partial(_out_proj_kernel, n_lhs=len(lhs)),
        grid=(n // tm, d // tn),
        in_specs=in_specs,
        out_specs=pl.BlockSpec((tm, tn), lambda i, j: (i, j)),
        out_shape=jax.ShapeDtypeStruct((n, d), F32),
        compiler_params=_params("parallel", "parallel"),
        name=name,
    )(*lhs, *ws, res)


def _router_kernel(x_ref, g_ref, wh_ref, wl_ref, logit_ref):
    xn = _norm_rows(x_ref[...], g_ref[...])
    xh = xn.astype(BF16)
    xl = (xn - xh.astype(F32)).astype(BF16)
    wh = wh_ref[...]
    logit_ref[...] = (jnp.dot(xh, wh, preferred_element_type=F32)
                      + (jnp.dot(xl, wh, preferred_element_type=F32)
                         + jnp.dot(xh, wl_ref[...], preferred_element_type=F32)))


def router_logits(h, gain, router_padded):
    n, d = h.shape
    tm = _row_tile(n)
    ne = router_padded.shape[1]
    w_hi = router_padded.astype(BF16)
    w_lo = (router_padded - w_hi.astype(F32)).astype(BF16)
    return pl.pallas_call(
        _router_kernel,
        grid=(n // tm,),
        in_specs=[pl.BlockSpec((tm, d), lambda i: (i, 0)),
                  pl.BlockSpec((1, d), lambda i: (0, 0)),
                  pl.BlockSpec((d, ne), lambda i: (0, 0)),
                  pl.BlockSpec((d, ne), lambda i: (0, 0))],
        out_specs=pl.BlockSpec((tm, ne), lambda i: (i, 0)),
        out_shape=jax.ShapeDtypeStruct((n, ne), F32),
        compiler_params=_params("parallel"),
        name="router_logits",
    )(h, gain.reshape(1, d).astype(F32), w_hi, w_lo)


def _final_norm_kernel(x_ref, y0_ref, y1_ref, g_ref, o_ref):
    o_ref[...] = _norm_rows(x_ref[...] + y0_ref[...] + y1_ref[...], g_ref[...])


def final_rmsnorm(h, addends, gain, b, l):
    n, d = h.shape
    seq = l - N_META
    tr = _pick(seq, (512, 256, 128, 64, 32, 16))

    def rows(offset):
        def index(bi, s):
            return pl.multiple_of(offset + bi * l + N_META + s * tr, N_META), 0
        return pl.BlockSpec((pl.Element(tr), pl.Element(d)), index)

    return pl.pallas_call(
        _final_norm_kernel,
        grid=(b, seq // tr),
        in_specs=[rows(0), rows(0), rows(n), pl.BlockSpec((1, d), lambda bi, s: (0, 0))],
        out_specs=pl.BlockSpec((None, tr, d), lambda bi, s: (bi, s, 0)),
        out_shape=jax.ShapeDtypeStruct((b, seq, d), F32),
        compiler_params=_params("parallel", "parallel"),
        name="final_norm",
    )(h, addends, addends, gain.reshape(1, d).astype(F32))


def _swiglu_part(xn, wg, wu, wd):
    g = jnp.dot(xn, wg, preferred_element_type=F32)
    u = jnp.dot(xn, wu, preferred_element_type=F32)
    a = (g * jax.nn.sigmoid(g) * u).astype(BF16)
    return jnp.dot(a, wd, preferred_element_type=F32)


def _dense_ffn_kernel(x_ref, g_ref, wg_ref, wu_ref, wd_ref, o_ref, xn_ref):
    j = pl.program_id(1)

    @pl.when(j == 0)
    def _():
        x = x_ref[...]
        xn_ref[...] = _norm_rows(x, g_ref[...]).astype(BF16)
        o_ref[...] = x

    o_ref[...] += _swiglu_part(xn_ref[...], wg_ref[...], wu_ref[...], wd_ref[...])


def dense_ffn(h, gain, w_gate_up, w_down, layer):
    n, d = h.shape
    dff = w_down.shape[1]
    tm = _row_tile(n)
    tf = _pick(dff, (512, 256, 128))
    nj = dff // tf
    return pl.pallas_call(
        _dense_ffn_kernel,
        grid=(n // tm, nj),
        in_specs=[pl.BlockSpec((tm, d), lambda i, j: (i, 0)),
                  pl.BlockSpec((1, d), lambda i, j: (0, 0)),
                  pl.BlockSpec((None, d, tf), lambda i, j: (layer, 0, j)),
                  pl.BlockSpec((None, d, tf), lambda i, j: (layer, 0, nj + j)),
                  pl.BlockSpec((None, tf, d), lambda i, j: (layer, j, 0))],
        out_specs=pl.BlockSpec((tm, d), lambda i, j: (i, 0)),
        out_shape=jax.ShapeDtypeStruct((n, d), F32),
        scratch_shapes=[pltpu.VMEM((tm, d), BF16)],
        compiler_params=_params("parallel", "arbitrary"),
        name="dense_ffn",
    )(h, gain.reshape(1, d).astype(F32), w_gate_up, w_gate_up, w_down)


def _moe_ffn_kernel(be_ref, nused_ref, tok_ref, dst_ref,
                    h_hbm, gain_ref, gate_ref, wg_ref, wu_ref, wd_ref, y_hbm,
                    xbuf, xn_ref, acc_ref, ybuf, gsem, ssem, *, tm, nj, rows_per_step):
    b = pl.program_id(0)
    j = pl.program_id(1)
    n_used = nused_ref[0]

    def gather_rows(blk, lo, hi):
        slot = blk % 2

        def body(r, c):
            tok = tok_ref[blk * tm + r]
            pltpu.make_async_copy(h_hbm.at[pl.ds(tok, 1)], xbuf.at[slot, pl.ds(r, 1)],
                                  gsem.at[slot]).start()
            return c

        lax.fori_loop(lo, hi, body, 0)

    def scatter_rows(blk, lo, hi):
        def body(r, c):
            dst = dst_ref[blk * tm + r]
            pltpu.make_async_copy(ybuf.at[pl.ds(r, 1)], y_hbm.at[pl.ds(dst, 1)], ssem.at[0]).start()
            return c

        lax.fori_loop(lo, hi, body, 0)

    def wait_scatter():
        pltpu.make_async_copy(ybuf, y_hbm.at[pl.ds(0, tm)], ssem.at[0]).wait()

    def wait_gather(slot):
        pltpu.make_async_copy(h_hbm.at[pl.ds(0, tm)], xbuf.at[slot], gsem.at[slot]).wait()

    trash0 = y_hbm.shape[0] - tm
    nxt = jnp.minimum(b + 1, n_used - 1)
    prv = jnp.maximum(b - 1, 0)

    def move_rows(rows):
        for r in rows:
            pltpu.make_async_copy(h_hbm.at[pl.ds(tok_ref[nxt * tm + r], 1)],
                                  xbuf.at[(b + 1) % 2, pl.ds(r, 1)], gsem.at[(b + 1) % 2]).start()
            dst = jnp.where(b > 0, dst_ref[prv * tm + r], trash0 + r)
            pltpu.make_async_copy(ybuf.at[pl.ds(r, 1)], y_hbm.at[pl.ds(dst, 1)], ssem.at[0]).start()

    @pl.when(b < n_used)
    def _():
        @pl.when(jnp.logical_and(b == 0, j == 0))
        def _():
            gather_rows(0, 0, tm)
            ybuf[...] = jnp.zeros_like(ybuf)
            init = pltpu.make_async_copy(ybuf, y_hbm.at[pl.ds(trash0, tm)], ssem.at[0])
            init.start()
            init.wait()

        @pl.when(j == 0)
        def _():
            wait_gather(b % 2)
            xn_ref[...] = _norm_rows(xbuf[b % 2], gain_ref[...]).astype(BF16)
            acc_ref[...] = jnp.zeros_like(acc_ref)
            move_rows([nj * rows_per_step + r for r in range(tm - nj * rows_per_step)])

        move_rows([j * rows_per_step + k for k in range(rows_per_step)])
        acc_ref[...] += _swiglu_part(xn_ref[...], wg_ref[0], wu_ref[0], wd_ref[0])

        @pl.when(j == nj - 1)
        def _():
            wait_scatter()
            ybuf[...] = acc_ref[...] * gate_ref[...]

            @pl.when(b == n_used - 1)
            def _():
                wait_gather((b + 1) % 2)
                scatter_rows(b, 0, tm)
                wait_scatter()


def moe_ffn(h, gain, blk_e, n_used, row_tok, row_dst, row_gate, w_gate, w_up, w_down, tm):
    n, d = h.shape
    dff = w_down.shape[1]
    tf = _pick(dff, (1024, 512, 256, 128))
    nj = dff // tf
    nb = row_tok.shape[0] // tm
    rows_per_step = tm // nj

    def jj(b, j, nused):
        return jnp.where(b < nused[0], j, nj - 1)

    grid_spec = pltpu.PrefetchScalarGridSpec(
        num_scalar_prefetch=4,
        grid=(nb, nj),
        in_specs=[pl.BlockSpec(memory_space=pl.ANY),
                  pl.BlockSpec((1, d), lambda b, j, be, nu, *_: (0, 0)),
                  pl.BlockSpec((tm, 1), lambda b, j, be, nu, *_: (b, 0)),
                  pl.BlockSpec((1, d, tf), lambda b, j, be, nu, *_: (be[b], 0, jj(b, j, nu))),
                  pl.BlockSpec((1, d, tf), lambda b, j, be, nu, *_: (be[b], 0, jj(b, j, nu))),
                  pl.BlockSpec((1, tf, d), lambda b, j, be, nu, *_: (be[b], jj(b, j, nu), 0))],
        out_specs=pl.BlockSpec(memory_space=pl.ANY),
        scratch_shapes=[pltpu.VMEM((2, tm, d), F32), pltpu.VMEM((tm, d), BF16),
                        pltpu.VMEM((tm, d), F32), pltpu.VMEM((tm, d), F32),
                        pltpu.SemaphoreType.DMA((2,)), pltpu.SemaphoreType.DMA((1,))],
    )
    return pl.pallas_call(
        functools.partial(_moe_ffn_kernel, tm=tm, nj=nj, rows_per_step=rows_per_step),
        grid_spec=grid_spec,
        out_shape=jax.ShapeDtypeStruct((TOP_K * n + tm, d), F32),
        compiler_params=_params("arbitrary", "arbitrary"),
        name="moe_ffn",
    )(blk_e, n_used, row_tok, row_dst, h, gain.reshape(1, d).astype(F32),
      row_gate.reshape(-1, 1), w_gate, w_up, w_down)


def _qk(q, k):
    return lax.dot_general(q, k, (((1,), (1,)), ((), ())), preferred_element_type=F32)


def _pv(p, v):
    return jnp.dot(p.astype(BF16), v, preferred_element_type=F32)


def _flash_update(s, v, m, l, acc):
    m_new = jnp.maximum(m, jnp.max(s, axis=-1, keepdims=True))
    alpha = jnp.exp2(m - m_new)
    p = jnp.exp2(s - m_new)
    l = alpha * l + jnp.sum(p, axis=-1, keepdims=True)
    acc = alpha * acc + _pv(p, v)
    return m_new, l, acc


def _iota2(shape):
    return (lax.broadcasted_iota(jnp.int32, shape, 0), lax.broadcasted_iota(jnp.int32, shape, 1))


def _query_tile(seq):
    return _pick(seq, (512, 256, 128))


def _causal_sweep(heads, nq, tq):
    r16, c16 = _iota2((N_META, BLOCK))
    meta_kv = [(k_ref[0:BLOCK, :], v_ref[0:BLOCK, :]) for _, k_ref, v_ref, _, _ in heads]

    for (q_ref, _, _, o_ref, col_bias), (k0, v0) in zip(heads, meta_kv):
        s = _qk(q_ref[0:N_META, :], k0) + col_bias(None)
        s = jnp.where(c16 <= r16, s, NEG)
        p = jnp.exp2(s - jnp.max(s, axis=-1, keepdims=True))
        o = _pv(p, v0) * (1.0 / jnp.sum(p, axis=-1, keepdims=True))
        o_ref[0:N_META, :] = o.astype(o_ref.dtype)

    def qblock(i, carry):
        qs = pl.multiple_of(N_META + i * tq, N_META)
        cm = lax.broadcasted_iota(jnp.int32, (tq, BLOCK), 1)
        r, c = _iota2((tq, tq))
        qs_ = []
        state = []
        for (q_ref, k_ref, v_ref, _, col_bias), (k0, v0) in zip(heads, meta_kv):
            q = q_ref[pl.ds(qs, tq), :]
            s0 = jnp.where(cm < N_META, _qk(q, k0) + col_bias(None), NEG)
            s1 = jnp.where(c <= r, _qk(q, k_ref[pl.ds(qs, tq), :]) + col_bias(i), NEG)
            m = jnp.maximum(jnp.max(s0, axis=-1, keepdims=True), jnp.max(s1, axis=-1, keepdims=True))
            p0 = jnp.exp2(s0 - m)
            p1 = jnp.exp2(s1 - m)
            l = jnp.sum(p0, axis=-1, keepdims=True) + jnp.sum(p1, axis=-1, keepdims=True)
            qs_.append(q)
            state.append((m, l, _pv(p0, v0) + _pv(p1, v_ref[pl.ds(qs, tq), :])))

        def chunk(t, state):
            ks = pl.multiple_of(N_META + t * tq, N_META)
            new = []
            for (_, k_ref, v_ref, _, col_bias), q, mla in zip(heads, qs_, state):
                s = _qk(q, k_ref[pl.ds(ks, tq), :]) + col_bias(t)
                new.append(_flash_update(s, v_ref[pl.ds(ks, tq), :], *mla))
            return tuple(new)

        state = lax.fori_loop(0, i, chunk, tuple(state))
        for (_, _, _, o_ref, _), (m, l, acc) in zip(heads, state):
            o_ref[pl.ds(qs, tq), :] = (acc * (1.0 / l)).astype(o_ref.dtype)
        return carry

    lax.fori_loop(0, nq, qblock, 0)


def _fox_kernel(q_ref, k_ref, v_ref, cm_ref, cr_ref, o_ref, *, nq, tq):
    def head(g):
        cols = slice(g * FOX_DIM, (g + 1) * FOX_DIM)

        def col_bias(t):
            return -cm_ref[g] if t is None else -cr_ref[g, pl.ds(t, 1), :]

        return (q_ref.at[:, cols], k_ref.at[:, cols], v_ref.at[:, cols], o_ref.at[:, cols], col_bias)

    _causal_sweep([head(g) for g in range(HEADS_PER_STEP)], nq, tq)


def fox_attention(proj, c_meta, c_real, b, l):
    nq, tq = c_real.shape[2:]
    h = FOX_HEADS // HEADS_PER_STEP
    gw = HEADS_PER_STEP * FOX_DIM
    p3 = proj.reshape(b, l, 3 * FOX_HEADS * FOX_DIM)
    blk = lambda off: pl.BlockSpec((None, l, gw), lambda bi, hi: (bi, 0, off + hi))
    out = pl.pallas_call(
        functools.partial(_fox_kernel, nq=nq, tq=tq),
        grid=(b, h),
        in_specs=[blk(0), blk(h), blk(2 * h),
                  pl.BlockSpec((None, HEADS_PER_STEP, 1, BLOCK), lambda bi, hi: (bi, hi, 0, 0)),
                  pl.BlockSpec((None, HEADS_PER_STEP, nq, tq), lambda bi, hi: (bi, hi, 0, 0))],
        out_specs=pl.BlockSpec((None, l, gw), lambda bi, hi: (bi, 0, hi)),
        out_shape=jax.ShapeDtypeStruct((b, l, FOX_HEADS * FOX_DIM), BF16),
        compiler_params=_params("parallel", "parallel"),
        name="fox_attention",
    )(p3, p3, p3, c_meta, c_real)
    return out.reshape(b * l, FOX_HEADS * FOX_DIM)


def _fox_decay_kernel(g_ref, bf_ref, c_ref, *, nchunk):
    r, c = _iota2((BLOCK, BLOCK))
    tri = jnp.where(c <= r, 1.0, 0.0).astype(F32)
    bf = bf_ref[...]

    def log_sigmoid(x):
        return jnp.minimum(x, 0.0) - jnp.log1p(jnp.exp(-jnp.abs(x)))

    def chunk_cumsum(start):
        lf = log_sigmoid(g_ref[pl.ds(start, BLOCK), :] + bf)
        return jnp.dot(tri, lf, preferred_element_type=F32, precision=lax.Precision.HIGHEST)

    head = chunk_cumsum(0)[0:N_META, :]
    c_ref[0:N_META, :] = head * LOG2E

    def body(t, carry):
        start = pl.multiple_of(N_META + t * BLOCK, N_META)
        cc = chunk_cumsum(start) + carry
        c_ref[pl.ds(start, BLOCK), :] = cc * LOG2E
        return cc[BLOCK - 1:BLOCK, :]

    lax.fori_loop(0, nchunk, body, head[N_META - 1:N_META, :])


def fox_decay(gate_logit, b_f_padded, b, l):
    g3 = gate_logit.reshape(b, l, LANES)
    return pl.pallas_call(
        functools.partial(_fox_decay_kernel, nchunk=(l - N_META) // BLOCK),
        grid=(b,),
        in_specs=[pl.BlockSpec((None, l, LANES), lambda bi: (bi, 0, 0)),
                  pl.BlockSpec((1, LANES), lambda bi: (0, 0))],
        out_specs=pl.BlockSpec((None, l, LANES), lambda bi: (bi, 0, 0)),
        out_shape=jax.ShapeDtypeStruct((b, l, LANES), F32),
        compiler_params=_params("parallel"),
        name="fox_decay",
    )(g3, b_f_padded)


def _mla_kernel(q_ref, kv_ref, kr_ref, cs_ref, o_ref, qx_ref, kx_ref, *, nq, tq):
    cs = cs_ref[...]
    lane = lax.broadcasted_iota(jnp.int32, cs.shape, 1)
    ek = kr_ref[...].astype(F32) * cs
    rk = jnp.where(lane < QK_ROPE, ek + pltpu.roll(ek, QK_ROPE, axis=1), 0.0).astype(BF16)
    w = QK_NOPE + 2 * QK_ROPE
    heads = []
    for g in range(HEADS_PER_STEP):
        eq = q_ref[:, g * w + QK_NOPE:(g + 1) * w].astype(F32) * cs
        qx_ref[g, :, :QK_NOPE] = q_ref[:, g * w:g * w + QK_NOPE]
        qx_ref[g, :, QK_NOPE:] = (eq + pltpu.roll(eq, QK_ROPE, axis=1)).astype(BF16)
        kx_ref[g, :, :QK_NOPE] = kv_ref[:, g * w:g * w + QK_NOPE]
        kx_ref[g, :, QK_NOPE:] = rk
        heads.append((qx_ref.at[g], kx_ref.at[g], kv_ref.at[:, g * w + QK_NOPE:(g + 1) * w],
                      o_ref.at[:, g * V_DIM:(g + 1) * V_DIM], lambda t: 0.0))
    _causal_sweep(heads, nq, tq)


def mla_attention(qa, kv, proj, kr_col, cs, b, l):
    tq = _query_tile(l - N_META)
    nq = (l - N_META) // tq
    h = MLA_HEADS // HEADS_PER_STEP
    w = QK_NOPE + 2 * QK_ROPE
    gw = HEADS_PER_STEP * w
    qa3 = qa.reshape(b, l, MLA_HEADS * w)
    kv3 = kv.reshape(b, l, MLA_HEADS * w)
    p3 = proj.reshape(b, l, proj.shape[1])
    out = pl.pallas_call(
        functools.partial(_mla_kernel, nq=nq, tq=tq),
        grid=(b, h),
        in_specs=[pl.BlockSpec((None, l, gw), lambda bi, hi: (bi, 0, hi)),
                  pl.BlockSpec((None, l, gw), lambda bi, hi: (bi, 0, hi)),
                  pl.BlockSpec((None, l, 2 * QK_ROPE), lambda bi, hi: (bi, 0, kr_col)),
                  pl.BlockSpec((l, 2 * QK_ROPE), lambda bi, hi: (0, 0))],
        out_specs=pl.BlockSpec((None, l, HEADS_PER_STEP * V_DIM), lambda bi, hi: (bi, 0, hi)),
        out_shape=jax.ShapeDtypeStruct((b, l, MLA_HEADS * V_DIM), BF16),
        scratch_shapes=[pltpu.VMEM((HEADS_PER_STEP, l, w), BF16),
                        pltpu.VMEM((HEADS_PER_STEP, l, w), BF16)],
        compiler_params=_params("parallel", "parallel"),
        name="mla_attention",
    )(qa3, kv3, p3, cs)
    return out.reshape(b * l, MLA_HEADS * V_DIM)


def _swa_kernel(sink_ref, slope_ref, q_ref, k_ref, v_ref, o_ref, *, nb):
    kvh = pl.program_id(1)
    r16, c16 = _iota2((N_META, BLOCK))
    rel16 = (r16 - c16).astype(F32)
    r3, c3 = _iota2((BLOCK, 3 * BLOCK))
    in_meta = c3 < BLOCK
    in_prev = jnp.logical_and(c3 >= BLOCK, c3 < 2 * BLOCK)
    rel3 = (r3 - c3).astype(F32)
    allowed = jnp.logical_or(
        jnp.logical_or(c3 < N_META, jnp.logical_and(in_prev, c3 - BLOCK > r3)),
        jnp.logical_and(c3 >= 2 * BLOCK, c3 - 2 * BLOCK <= r3))
    mask_bias = jnp.where(allowed, 0.0, NEG)
    prev_bias = jnp.where(in_prev, NEG, 0.0)
    k0 = k_ref[0:BLOCK, :]
    v0 = v_ref[0:BLOCK, :]
    sinks = [sink_ref[kvh * SWA_GROUP + g] for g in range(SWA_GROUP)]
    slopes = [slope_ref[kvh * SWA_GROUP + g] for g in range(SWA_GROUP)]

    for g in range(SWA_GROUP):
        cols = slice(g * SWA_DIM, (g + 1) * SWA_DIM)
        s = _qk(q_ref[0:N_META, cols], k0) - slopes[g] * rel16
        s = jnp.where(c16 <= r16, s, NEG)
        m = jnp.maximum(jnp.max(s, axis=-1, keepdims=True), sinks[g])
        p = jnp.exp2(s - m)
        l = jnp.sum(p, axis=-1, keepdims=True) + jnp.exp2(sinks[g] - m)
        o_ref[0:N_META, cols] = (_pv(p, v0) * (1.0 / l)).astype(o_ref.dtype)

    def qblock(i, carry):
        qs = pl.multiple_of(N_META + i * BLOCK, N_META)
        ps = pl.multiple_of(jnp.where(i > 0, qs - BLOCK, qs), N_META)
        kcat = jnp.concatenate([k0, k_ref[pl.ds(ps, BLOCK), :], k_ref[pl.ds(qs, BLOCK), :]], axis=0)
        vcat = jnp.concatenate([v0, v_ref[pl.ds(ps, BLOCK), :], v_ref[pl.ds(qs, BLOCK), :]], axis=0)
        dist = rel3 + jnp.where(in_meta, jnp.asarray(N_META + i * BLOCK, F32), float(2 * BLOCK))
        bias = mask_bias + prev_bias * jnp.where(i > 0, 0.0, 1.0)
        for g in range(SWA_GROUP):
            cols = slice(g * SWA_DIM, (g + 1) * SWA_DIM)
            s = _qk(q_ref[pl.ds(qs, BLOCK), cols], kcat) + (bias - slopes[g] * dist)
            m = jnp.maximum(jnp.max(s, axis=-1, keepdims=True), sinks[g])
            p = jnp.exp2(s - m)
            l = jnp.sum(p, axis=-1, keepdims=True) + jnp.exp2(sinks[g] - m)
            o_ref[pl.ds(qs, BLOCK), cols] = (_pv(p, vcat) * (1.0 / l)).astype(o_ref.dtype)
        return carry

    lax.fori_loop(0, nb, qblock, 0, unroll=4 if nb % 4 == 0 else 1)


def swa_attention(proj, q_col, k_col, v_col, sinks, slopes, b, l):
    nb = (l - N_META) // BLOCK
    gw = SWA_GROUP * SWA_DIM
    p3 = proj.reshape(b, l, proj.shape[1])
    grid_spec = pltpu.PrefetchScalarGridSpec(
        num_scalar_prefetch=2,
        grid=(b, SWA_KV_HEADS),
        in_specs=[pl.BlockSpec((None, l, gw), lambda bi, hi, s0, s1: (bi, 0, q_col + hi)),
                  pl.BlockSpec((None, l, SWA_DIM), lambda bi, hi, s0, s1: (bi, 0, k_col + hi)),
                  pl.BlockSpec((None, l, SWA_DIM), lambda bi, hi, s0, s1: (bi, 0, v_col + hi))],
        out_specs=pl.BlockSpec((None, l, gw), lambda bi, hi, s0, s1: (bi, 0, hi)),
    )
    out = pl.pallas_call(
        functools.partial(_swa_kernel, nb=nb),
        grid_spec=grid_spec,
        out_shape=jax.ShapeDtypeStruct((b, l, SWA_HEADS * SWA_DIM), BF16),
        compiler_params=_params("parallel", "parallel"),
        name="swa_attention",
    )(sinks, slopes, p3, p3, p3)
    return out.reshape(b * l, SWA_HEADS * SWA_DIM)


def _rope_swap_cols(w):
    half = QK_ROPE // 2
    return jnp.concatenate([-w[..., half:], w[..., :half]], axis=-1)


def _rope_table(l):
    half = QK_ROPE // 2
    inv = ROPE_THETA ** (-jnp.arange(half, dtype=F32) / half)
    ang = jnp.arange(l, dtype=F32)[:, None] * inv[None, :]
    cos, sin = jnp.cos(ang), jnp.sin(ang)
    return jnp.concatenate([cos, cos, sin, sin], axis=-1)


def even_layer(h, pending, b, l, layer, norm1, w_in, q_norm, kv_norm, w_uq, w_ukv, sinks, w_o, norm2,
               w_gate_up, w_down, cs, slopes):
    d = h.shape[1]
    o1, o2, o3 = Q_LORA, Q_LORA + KV_LORA, Q_LORA + KV_LORA + QK_ROPE
    o4 = o3 + SWA_HEADS * SWA_DIM
    o5 = o4 + SWA_KV_HEADS * SWA_DIM
    w_kr = w_in[:, o2:o3]
    w_in2 = jnp.concatenate([w_in[:, o3:o4] * (SWA_DIM ** -0.5 * LOG2E), w_in[:, :o1], w_in[:, o1:o2],
                             w_in[:, o4:o5], w_in[:, o5:], w_kr, _rope_swap_cols(w_kr)],
                            axis=1).astype(BF16)
    w_in2 = jnp.pad(w_in2, ((0, 0), (0, -w_in2.shape[1] % 512)))
    res = norm_proj(h, 0, d, norm1, w_in2, addends=pending, name="even_in_proj")
    proj = res[0]
    if pending is not None:
        h = res[1]
    cq_col = (SWA_HEADS * SWA_DIM) // Q_LORA
    ckv_col = (SWA_HEADS * SWA_DIM + Q_LORA) // KV_LORA
    ks_col = (SWA_HEADS * SWA_DIM + Q_LORA + KV_LORA) // SWA_DIM
    vs_col = ks_col + SWA_KV_HEADS
    kr_col = vs_col + SWA_KV_HEADS

    wq = w_uq.reshape(Q_LORA, MLA_HEADS, QK_NOPE + QK_ROPE) * ((QK_NOPE + QK_ROPE) ** -0.5 * LOG2E)
    wq2 = jnp.concatenate([wq, _rope_swap_cols(wq[..., QK_NOPE:])], axis=-1)
    wq2 = wq2.reshape(Q_LORA, MLA_HEADS * (QK_NOPE + 2 * QK_ROPE)).astype(BF16)
    qa = norm_proj(proj, cq_col, Q_LORA, q_norm, wq2, name="mla_q_up")[0]
    kv = norm_proj(proj, ckv_col, KV_LORA, kv_norm, w_ukv.astype(BF16), name="mla_kv_up")[0]

    o_a = mla_attention(qa, kv, proj, kr_col, cs, b, l)
    o_b = swa_attention(proj, 0, ks_col, vs_col, sinks.astype(F32) * LOG2E, slopes * LOG2E, b, l)
    na = MLA_HEADS * V_DIM
    h = out_proj([o_a, o_b], [w_o[:na].astype(BF16), w_o[na:].astype(BF16)], h, name="even_out_proj")
    return dense_ffn(h, norm2, w_gate_up, w_down, layer)


def moe_layer(h, layer, norm2, router, w_gate, w_up, w_down):
    n, d = h.shape
    ne = router.shape[1]
    router_p = jnp.zeros((d, LANES), F32).at[:, :ne].set(router)
    logits = router_logits(h, norm2, router_p)
    top_val, top_idx = lax.top_k(logits[:, :ne], TOP_K)
    gates = jax.nn.softmax(top_val, axis=-1)
    a = n * TOP_K
    tm = min(MOE_ROWS, -(-(a // (8 * ne)) // BF16_ROWS) * BF16_ROWS)
    flat_e = top_idx.reshape(-1).astype(jnp.int32)
    _, s_pair, s_gate = lax.sort((flat_e, jnp.arange(a, dtype=jnp.int32), gates.reshape(-1)),
                                 num_keys=1, is_stable=True)
    counts = jnp.sum(flat_e[:, None] == jnp.arange(ne, dtype=jnp.int32)[None, :], axis=0)
    padded = (counts + tm - 1) // tm * tm
    start_sorted = jnp.cumsum(counts) - counts
    end_pad = jnp.cumsum(padded)
    start_pad = end_pad - padded
    n_blocks = -(-a // tm) + ne
    p = n_blocks * tm
    blk_start = jnp.arange(n_blocks) * tm
    blk_e = jnp.minimum(jnp.sum(blk_start[:, None] >= end_pad[None, :], axis=1), ne - 1)
    n_used = (end_pad[-1] // tm).astype(jnp.int32).reshape(1)
    in_blk = jnp.arange(tm, dtype=jnp.int32)[None, :]
    rank = (blk_start - start_pad[blk_e]).astype(jnp.int32)[:, None] + in_blk
    valid = (rank < counts[blk_e][:, None]).reshape(p)
    src = jnp.where(valid, (start_sorted[blk_e].astype(jnp.int32)[:, None] + rank).reshape(p), 0)
    pair = s_pair[src]
    trash = jnp.broadcast_to(TOP_K * n + in_blk, (n_blocks, tm)).reshape(p)
    row_tok = jnp.where(valid, pair // TOP_K, 0).astype(jnp.int32)
    row_dst = jnp.where(valid, (pair % TOP_K) * n + pair // TOP_K, trash).astype(jnp.int32)
    row_gate = jnp.where(valid, s_gate[src], 0.0)
    return moe_ffn(h, norm2, (blk_e + layer * ne).astype(jnp.int32), n_used, row_tok, row_dst,
                   row_gate, w_gate, w_up, w_down, tm)


def odd_layer(h, b, l, layer, norm1, w_in, b_f, w_o, norm2, router, w_gate, w_up, w_down):
    d = h.shape[1]
    hd = FOX_HEADS * FOX_DIM
    tq = _query_tile(l - N_META)
    w_qkv = jnp.concatenate([w_in[:, :hd] * (FOX_DIM ** -0.5 * LOG2E), w_in[:, hd:3 * hd]],
                            axis=1).astype(BF16)
    w_f = jnp.zeros((d, LANES), F32).at[:, :FOX_HEADS].set(w_in[:, 3 * hd:]).astype(BF16)
    proj, gate_logit = norm_proj(h, 0, d, norm1, w_qkv, w_aux=w_f, name="fox_in_proj")
    b_f_p = jnp.zeros((1, LANES), F32).at[0, :FOX_HEADS].set(b_f.astype(F32))
    c = fox_decay(gate_logit, b_f_p, b, l)
    c = jnp.swapaxes(c[:, :, :FOX_HEADS], 1, 2)
    c_meta = jnp.zeros((b, FOX_HEADS, 1, BLOCK), F32).at[:, :, 0, :N_META].set(c[:, :, :N_META])
    c_real = c[:, :, N_META:].reshape(b, FOX_HEADS, (l - N_META) // tq, tq)
    o = fox_attention(proj, c_meta, c_real, b, l)
    h = out_proj([o], [w_o.astype(BF16)], h, name="fox_out_proj")
    return h, moe_layer(h, layer, norm2, router, w_gate, w_up, w_down)


def alibi_slopes(n):
    return jnp.asarray([2.0 ** (-8.0 * (i + 1) / n) for i in range(n)], dtype=F32)


def kernel(x, meta, ev_norm1, ev_w_in, ev_q_norm, ev_kv_norm, ev_w_uq, ev_w_ukv, ev_sinks, ev_w_o,
           ev_norm2, ev_w_gate_up, ev_w_down, od_norm1, od_w_in, od_b_f, od_w_o, od_norm2,
           od_router, od_w_gate, od_w_up, od_w_down, final_norm):
    b, seq, d = x.shape
    l = seq + N_META
    depth = ev_norm1.shape[0] + od_norm1.shape[0]
    assert depth % 2 == 0, "the trunk must end with an odd (MoE) layer"
    h = jnp.concatenate([jnp.broadcast_to(meta[None].astype(x.dtype), (b, N_META, d)), x], axis=1)
    h = h.reshape(b * l, d)
    cs = _rope_table(l)
    slopes = alibi_slopes(SWA_HEADS)
    ev_gu, ev_dn = ev_w_gate_up.astype(BF16), ev_w_down.astype(BF16)
    dff = od_w_gate.shape[-1]
    od_g = od_w_gate.astype(BF16).reshape(-1, d, dff)
    od_u = od_w_up.astype(BF16).reshape(-1, d, dff)
    od_d = od_w_down.astype(BF16).reshape(-1, dff, d)
    pending = None
    for layer in range(depth):
        i = layer // 2
        if layer % 2 == 0:
            h = even_layer(h, pending, b, l, i, ev_norm1[i], ev_w_in[i], ev_q_norm[i], ev_kv_norm[i],
                           ev_w_uq[i], ev_w_ukv[i], ev_sinks[i], ev_w_o[i], ev_norm2[i], ev_gu, ev_dn,
                           cs, slopes)
        else:
            h, pending = odd_layer(h, b, l, i, od_norm1[i], od_w_in[i], od_b_f[i], od_w_o[i],
                                   od_norm2[i], od_router[i], od_g, od_u, od_d)
    return final_rmsnorm(h, pending, final_norm, b, l)
```
